```python
import jax, jax.numpy as jnp
from jax import lax
import numpy as np

D_MODEL = 2048
BATCH = 4
SEQ = 2048
DEPTH = 4

HEAD_DIM = 128
SB_HEADS = D_MODEL // (2 * HEAD_DIM)
GDN_HEADS = D_MODEL // (2 * HEAD_DIM)
SB_WIDTH = SB_HEADS * HEAD_DIM
GDN_WIDTH = GDN_HEADS * HEAD_DIM
D_MIX = SB_WIDTH + GDN_WIDTH
N_IN = 3 * SB_WIDTH + 4 * GDN_WIDTH + 2 * GDN_HEADS
SB_BLOCK = 128
SHORT_CONV = 4
CHUNK = 64
N_MEM = 256
X_HEADS = 4
X_HEAD_DIM = 128
X_WIDTH = X_HEADS * X_HEAD_DIM
D_FF = (11 * D_MODEL) // 4
FFN_CONV = 3
EPS = 1e-6

kernel_name = "hybrid_stickbreak_gdn_memxattn_convffn"


def rmsnorm(x, g):
    xf = x.astype(jnp.float32)
    y = xf * lax.rsqrt(jnp.mean(xf * xf, axis=-1, keepdims=True) + EPS)
    return (y * g.astype(jnp.float32)).astype(x.dtype)


def l2norm(x):
    return x * lax.rsqrt(jnp.sum(x * x, axis=-1, keepdims=True) + EPS)


def causal_dwconv(x, w):
    k = w.shape[0]
    return lax.conv_general_dilated(
        x, w[:, None, :].astype(x.dtype), window_strides=(1,), padding=[(k - 1, 0)],
        dimension_numbers=('NWC', 'WIO', 'NWC'), feature_group_count=x.shape[-1])


def stick_breaking_attention(q, k, v):
    b, s, h, dh = q.shape
    scale = dh ** -0.5
    n_blk = s // SB_BLOCK
    qb = jnp.moveaxis(q.reshape(b, n_blk, SB_BLOCK, h, dh), 1, 0)
    key_pos = jnp.arange(s)

    def block(args):
        q_blk, blk_idx = args
        z = jnp.einsum('bqhd,bshd->bhqs', q_blk, k) * scale
        q_pos = blk_idx * SB_BLOCK + jnp.arange(SB_BLOCK)
        valid = key_pos[None, :] < q_pos[:, None]
        log_beta = jax.nn.log_sigmoid(z)
        log_stay = jnp.where(valid, jax.nn.log_sigmoid(-z), 0.0)
        later = lax.cumsum(log_stay, axis=3, reverse=True) - log_stay
        weights = jnp.where(valid, jnp.exp(log_beta + later), 0.0)
        return jnp.einsum('bhqs,bshd->bqhd', weights, v)

    out = lax.map(block, (qb, jnp.arange(n_blk)))
    return jnp.moveaxis(out, 0, 1).reshape(b, s, h, dh)


def gated_delta_rule(q, k, v, beta, g):
    b, s, h, dk = q.shape
    dv = v.shape[-1]
    n = s // CHUNK

    def chunks(t):
        t = jnp.swapaxes(t, 1, 2)
        return t.reshape(b, h, n, CHUNK, *t.shape[3:])

    q, k, v, beta, g = chunks(q), chunks(k), chunks(v), chunks(beta), chunks(g)
    g_cum = jnp.cumsum(g, axis=-1)
    idx = jnp.arange(CHUNK)
    lower_incl = idx[:, None] >= idx[None, :]
    strict = idx[:, None] > idx[None, :]
    decay = jnp.exp(jnp.where(lower_incl, g_cum[..., :, None] - g_cum[..., None, :], -jnp.inf))
    k_beta = k * beta[..., None]
    v_beta = v * beta[..., None]
    lmat = jnp.where(strict, jnp.einsum('bhnid,bhnjd->bhnij', k_beta, k) * decay, 0.0)
    eye = jnp.eye(CHUNK, dtype=jnp.float32)
    t_inv = lax.linalg.triangular_solve(eye + lmat, jnp.broadcast_to(eye, lmat.shape),
                                        left_side=True, lower=True, unit_diagonal=True)
    u = t_inv @ v_beta
    w = t_inv @ (k_beta * jnp.exp(g_cum)[..., None])
    attn_intra = jnp.where(lower_incl, jnp.einsum('bhnid,bhnjd->bhnij', q, k) * decay, 0.0)
    q_decay = q * jnp.exp(g_cum)[..., None]
    k_tail = k * jnp.exp(g_cum[..., -1:] - g_cum)[..., None]
    chunk_decay = jnp.exp(g_cum[..., -1])

    def step(state, xs):
        u_c, w_c, qd_c, a_c, kt_c, cd_c = xs
        v_new = u_c - w_c @ state
        o = qd_c @ state + a_c @ v_new
        state = state * cd_c[..., None, None] + jnp.swapaxes(kt_c, -1, -2) @ v_new
        return state, o

    xs = tuple(jnp.moveaxis(t, 2, 0) for t in (u, w, q_decay, attn_intra, k_tail, chunk_decay))
    state0 = jnp.zeros((b, h, dk, dv), jnp.float32)
    _, o = lax.scan(step, state0, xs)
    o = jnp.moveaxis(o, 0, 2).reshape(b, h, s, dv)
    return jnp.swapaxes(o, 1, 2)


def gated_deltanet(q, k, v, z, b_logit, a_logit, conv_w, a_log, dt_bias, norm_g):
    bsz, s, _ = q.shape
    qkv = jnp.concatenate([q, k, v], axis=-1).astype(jnp.float32)
    qkv = jax.nn.silu(causal_dwconv(qkv, conv_w.astype(jnp.float32)))
    q, k, v = jnp.split(qkv, 3, axis=-1)
    heads = lambda t: t.reshape(bsz, s, GDN_HEADS, HEAD_DIM)
    q = l2norm(heads(q)) * (HEAD_DIM ** -0.5)
    k = l2norm(heads(k))
    v = heads(v)
    beta = jax.nn.sigmoid(b_logit.astype(jnp.float32))
    g = -jnp.exp(a_log.astype(jnp.float32)) * jax.nn.softplus(
        a_logit.astype(jnp.float32) + dt_bias.astype(jnp.float32))
    o = gated_delta_rule(q, k, v, beta, g)
    o = rmsnorm(o, norm_g) * jax.nn.silu(heads(z.astype(jnp.float32)))
    return o.reshape(bsz, s, GDN_WIDTH)


def memory_cross_attention(h, mem_n, w_q, w_kv, w_o):
    bsz, s, _ = h.shape
    m = mem_n.shape[1]
    q = (h @ w_q).reshape(bsz, s, X_HEADS, X_HEAD_DIM)
    k, v = jnp.split(mem_n @ w_kv, 2, axis=-1)
    k = k.reshape(bsz, m, X_HEADS, X_HEAD_DIM)
    v = v.reshape(bsz, m, X_HEADS, X_HEAD_DIM)
    logits = jnp.einsum('bshd,bmhd->bhsm', q.astype(jnp.float32), k.astype(jnp.float32)) * (X_HEAD_DIM ** -0.5)
    p = jax.nn.softmax(logits, axis=-1)
    o = jnp.einsum('bhsm,bmhd->bshd', p, v.astype(jnp.float32))
    return o.reshape(bsz, s, X_WIDTH).astype(h.dtype) @ w_o


def conv_ffn(h, w_up, conv_w, conv_b, w_down):
    u = causal_dwconv(h @ w_up, conv_w) + conv_b
    gate, up = jnp.split(u, 2, axis=-1)
    return (jax.nn.silu(gate) * up) @ w_down


def setup_inputs(seed: int = 0) -> dict:
    key = jax.random.key(seed)
    ks = iter(jax.random.split(key, 32))
    f32 = jnp.float32
    nrm = lambda shape, scale: jax.random.normal(next(ks), shape, f32) * scale
    gain = lambda shape: 1.0 + 0.02 * jax.random.normal(next(ks), shape, f32)
    dt = jnp.exp(jax.random.uniform(next(ks), (DEPTH, GDN_HEADS), f32, np.log(1e-3), np.log(1e-1)))
    return {
        "x": jax.random.normal(next(ks), (BATCH, SEQ, D_MODEL), f32),
        "mem": jax.random.normal(next(ks), (BATCH, N_MEM, D_MODEL), f32),
        "mix_norm": gain((DEPTH, D_MODEL)),
        "w_in": nrm((DEPTH, D_MODEL, N_IN), D_MODEL ** -0.5),
        "gdn_conv": nrm((DEPTH, SHORT_CONV, 3 * GDN_WIDTH), SHORT_CONV ** -0.5),
        "gdn_a_log": jnp.log(jax.random.uniform(next(ks), (DEPTH, GDN_HEADS), f32, 1.0, 16.0)),
        "gdn_dt_bias": dt + jnp.log(-jnp.expm1(-dt)),
        "gdn_norm": gain((DEPTH, HEAD_DIM)),
        "w_out": nrm((DEPTH, D_MIX, D_MODEL), D_MIX ** -0.5),
        "xattn_norm": gain((DEPTH, D_MODEL)),
        "mem_norm": gain((DEPTH, D_MODEL)),
        "w_xq": nrm((DEPTH, D_MODEL, X_WIDTH), D_MODEL ** -0.5),
        "w_xkv": nrm((DEPTH, D_MODEL, 2 * X_WIDTH), D_MODEL ** -0.5),
        "w_xo": nrm((DEPTH, X_WIDTH, D_MODEL), X_WIDTH ** -0.5),
        "ffn_norm": gain((DEPTH, D_MODEL)),
        "w_up": nrm((DEPTH, D_MODEL, 2 * D_FF), D_MODEL ** -0.5),
        "ffn_conv": nrm((DEPTH, FFN_CONV, 2 * D_FF), FFN_CONV ** -0.5),
        "ffn_conv_bias": nrm((DEPTH, 2 * D_FF), 0.02),
        "w_down": nrm((DEPTH, D_FF, D_MODEL), D_FF ** -0.5),
        "final_norm": gain((D_MODEL,)),
    }


def reference(x, mem, mix_norm, w_in, gdn_conv, gdn_a_log, gdn_dt_bias, gdn_norm, w_out,
              xattn_norm, mem_norm, w_xq, w_xkv, w_xo, ffn_norm, w_up, ffn_conv,
              ffn_conv_bias, w_down, final_norm):
    bsz, s, _ = x.shape
    cuts = [SB_WIDTH, 2 * SB_WIDTH, 3 * SB_WIDTH,
            3 * SB_WIDTH + GDN_WIDTH, 3 * SB_WIDTH + 2 * GDN_WIDTH,
            3 * SB_WIDTH + 3 * GDN_WIDTH, 3 * SB_WIDTH + 4 * GDN_WIDTH,
            3 * SB_WIDTH + 4 * GDN_WIDTH + GDN_HEADS]
    sb_heads = lambda t: t.astype(jnp.float32).reshape(bsz, s, SB_HEADS, HEAD_DIM)
    for l in range(DEPTH):
        h = rmsnorm(x, mix_norm[l])
        proj = h @ w_in[l]
        sq, sk, sv, gq, gk, gv, gz, gb, ga = jnp.split(proj, cuts, axis=-1)
        sb_out = stick_breaking_attention(sb_heads(sq), sb_heads(sk), sb_heads(sv))
        sb_out = sb_out.reshape(bsz, s, SB_WIDTH)
        gdn_out = gated_deltanet(gq, gk, gv, gz, gb, ga, gdn_conv[l], gdn_a_log[l],
                                 gdn_dt_bias[l], gdn_norm[l])
        mixed = jnp.concatenate([sb_out, gdn_out], axis=-1).astype(x.dtype)
        x = x + mixed @ w_out[l]
        mem_n = rmsnorm(mem, mem_norm[l])
        x = x + memory_cross_attention(rmsnorm(x, xattn_norm[l]), mem_n, w_xq[l], w_xkv[l], w_xo[l])
        x = x + conv_ffn(rmsnorm(x, ffn_norm[l]), w_up[l], ffn_conv[l], ffn_conv_bias[l], w_down[l])
    return rmsnorm(x, final_norm)
```

```python
import functools
import math

import jax
import jax.numpy as jnp
from jax import lax
from jax.experimental import pallas as pl
from jax.experimental.pallas import tpu as pltpu

F32 = jnp.float32
BF16 = jnp.bfloat16

HEAD_DIM = 128
SB_HEADS = 8
GDN_HEADS = 8
SB_WIDTH = SB_HEADS * HEAD_DIM
GDN_WIDTH = GDN_HEADS * HEAD_DIM
SHORT_CONV = 4
CHUNK = 64
X_HEADS = 4
X_WIDTH = X_HEADS * HEAD_DIM
FFN_CONV = 3
EPS = 1e-6

V7X_VMEM_BYTES = 64 * 1024 * 1024
VMEM_LIMIT_BYTES = V7X_VMEM_BYTES - 8 * 1024 * 1024
NORM_ROWS = 64
HALO_ROWS = 16


def _params(n_axes):
    return pltpu.CompilerParams(
        dimension_semantics=("arbitrary",) * n_axes, vmem_limit_bytes=VMEM_LIMIT_BYTES)


def _rmsnorm(x, g):
    ms = jnp.mean(x * x, axis=-1, keepdims=True)
    return x * lax.rsqrt(ms + EPS) * g


def _rmsnorm_into(dst_ref, dst_row0, x_ref, g, n_rows, scale=None):
    step = min(NORM_ROWS, n_rows)
    dst_align = math.gcd(step, dst_row0) if dst_row0 else step

    def body(r, carry):
        src = pl.ds(pl.multiple_of(r * step, step), step)
        dst = pl.ds(pl.multiple_of(dst_row0 + r * step, dst_align), step)
        y = _rmsnorm(x_ref[src, :], g)
        if scale is not None:
            y = y * scale
        dst_ref[dst, :] = y.astype(dst_ref.dtype)
        return carry

    lax.fori_loop(0, n_rows // step, body, 0)


def _sigmoid(x):
    return 1.0 / (1.0 + jnp.exp(-x))


def _silu(x):
    return x * _sigmoid(x)


def _softplus(x):
    return jnp.maximum(x, 0.0) + jnp.log1p(jnp.exp(-jnp.abs(x)))


def _dot(a, b):
    return jnp.dot(a, b, preferred_element_type=F32)


def _dot_nt(a, b):
    return lax.dot_general(a, b, (((1,), (1,)), ((), ())), preferred_element_type=F32)


def _dot_tn(a, b):
    return lax.dot_general(a, b, (((0,), (0,)), ((), ())), preferred_element_type=F32)


def _norm_matmul_kernel(x_ref, g_ref, w_ref, *rest, tm, has_side):
    if has_side:
        ws_ref, o_ref, os_ref, h_ref = rest
    else:
        o_ref, h_ref = rest

    @pl.when(pl.program_id(1) == 0)
    def _():
        _rmsnorm_into(h_ref, 0, x_ref, g_ref[...], tm)
        if has_side:
            os_ref[...] = _dot(h_ref[...], ws_ref[...]).astype(os_ref.dtype)

    o_ref[...] = _dot(h_ref[...], w_ref[...]).astype(o_ref.dtype)


def norm_matmul(x, g, w, layer, col0, n_cols, out_dtype, *, tm, tn, side_w=None, name):
    m, k = x.shape
    tm = min(tm, m)
    assert m % tm == 0 and n_cols % tn == 0 and col0 % tn == 0
    off = col0 // tn
    in_specs = [
        pl.BlockSpec((tm, k), lambda i, j: (i, 0)),
        pl.BlockSpec((None, 1, k), lambda i, j: (layer, 0, 0)),
        pl.BlockSpec((None, k, tn), lambda i, j: (layer, 0, j + off)),
    ]
    out_shape = [jax.ShapeDtypeStruct((m, n_cols), out_dtype)]
    out_specs = [pl.BlockSpec((tm, tn), lambda i, j: (i, j))]
    args = [x, g, w]
    if side_w is not None:
        ns = side_w.shape[-1]
        in_specs.append(pl.BlockSpec((None, k, ns), lambda i, j: (layer, 0, 0)))
        out_shape.append(jax.ShapeDtypeStruct((m, ns), F32))
        out_specs.append(pl.BlockSpec((tm, ns), lambda i, j: (i, 0)))
        args.append(side_w)
    out = pl.pallas_call(
        functools.partial(_norm_matmul_kernel, tm=tm, has_side=side_w is not None),
        grid=(m // tm, n_cols // tn),
        in_specs=in_specs, out_specs=out_specs, out_shape=out_shape,
        scratch_shapes=[pltpu.VMEM((tm, k), BF16)],
        compiler_params=_params(2), name=name,
    )(*args)
    return out if side_w is not None else out[0]


def _sb_attn_kernel(q_ref, k_ref, v_ref, o_ref, *, seq, tile):
    scale = HEAD_DIM ** -0.5
    row = lax.broadcasted_iota(jnp.int32, (tile, tile), 0)
    col = lax.broadcasted_iota(jnp.int32, (tile, tile), 1)
    suffix = jnp.where(row > col, 1.0, 0.0).astype(BF16)
    causal = col < row

    def tile_terms(q, kj, carry, masked):
        rows = pl.ds(pl.multiple_of(kj * tile, tile), tile)
        z = _dot_nt(q, k_ref[rows, :]) * scale
        log_beta = jnp.minimum(z, 0.0) - jnp.log1p(jnp.exp(-jnp.abs(z)))
        log_stay = log_beta - z
        if masked:
            log_stay = jnp.where(causal, log_stay, 0.0)
        hi = log_stay.astype(BF16)
        lo = (log_stay - hi.astype(F32)).astype(BF16)
        later = _dot(hi, suffix) + _dot(lo, suffix)
        if carry is not None:
            later = later + carry
        w = jnp.exp(log_beta + later)
        if masked:
            w = jnp.where(causal, w, 0.0)
        pv = _dot(w.astype(BF16), v_ref[rows, :])
        return pv, jnp.sum(log_stay, axis=-1, keepdims=True)

    def q_block(qi, c):
        qrows = pl.ds(pl.multiple_of(qi * tile, tile), tile)
        q = q_ref[qrows, :]
        acc, carry = tile_terms(q, qi, None, True)

        def k_step(s, state):
            acc, carry = state
            pv, tot = tile_terms(q, qi - 1 - s, carry, False)
            return acc + pv, carry + tot

        acc, carry = lax.fori_loop(0, qi, k_step, (acc, carry))
        o_ref[qrows, :] = acc.astype(o_ref.dtype)
        return c

    lax.fori_loop(0, seq // tile, q_block, 0)


def sb_attention(qkv, batch, seq, *, tile=256, name):
    tile = min(tile, seq)
    spec = lambda part: pl.BlockSpec((seq, HEAD_DIM), lambda b, h: (b, part * SB_HEADS + h))
    return pl.pallas_call(
        functools.partial(_sb_attn_kernel, seq=seq, tile=tile),
        grid=(batch, SB_HEADS),
        in_specs=[spec(0), spec(1), spec(2)],
        out_specs=pl.BlockSpec((seq, HEAD_DIM), lambda b, h: (b, h)),
        out_shape=jax.ShapeDtypeStruct((batch * seq, SB_WIDTH), BF16),
        compiler_params=_params(2), name=name,
    )(qkv, qkv, qkv)


def _gdn_kernel(alog_ref, dtb_ref, xq_ref, xk_ref, xv_ref, z_ref, cwq_ref, cwk_ref, cwv_ref,
                brow_ref, arow_ref, bcol_ref, acol_ref, ng_ref, o_ref,
                q_s, k_s, v_s, u_s, w_s, qd_s, kt_s, at_s, gcr_s, *, layer, seq):
    head = pl.program_id(1)
    n_chunks = seq // CHUNK
    conv_rows = min(256, seq)

    def conv_phase(x_ref, cw_ref, dst_ref, post):
        cw = cw_ref[...]

        def body(r, prev):
            rows = pl.ds(pl.multiple_of(r * conv_rows, conv_rows), conv_rows)
            xb = x_ref[rows, :]
            xx = jnp.concatenate([prev, xb], axis=0)
            y = cw[3:4, :] * xx[8:, :]
            for tap in range(SHORT_CONV - 1):
                shift = SHORT_CONV - 1 - tap
                y = y + cw[tap:tap + 1, :] * xx[8 - shift:8 - shift + conv_rows, :]
            dst_ref[rows, :] = post(_silu(y))
            return xb[conv_rows - 8:, :]

        lax.fori_loop(0, seq // conv_rows, body, jnp.zeros((8, HEAD_DIM), F32))

    def l2norm(y):
        return y * lax.rsqrt(jnp.sum(y * y, axis=-1, keepdims=True) + EPS)

    conv_phase(xq_ref, cwq_ref, q_s, lambda y: l2norm(y) * (HEAD_DIM ** -0.5))
    conv_phase(xk_ref, cwk_ref, k_s, l2norm)
    conv_phase(xv_ref, cwv_ref, v_s, lambda y: y)

    neg_a = -jnp.exp(jnp.full((1, 1), alog_ref[layer, head], F32))
    dtb = dtb_ref[layer, head]
    ri = lax.broadcasted_iota(jnp.int32, (CHUNK, CHUNK), 0)
    ci = lax.broadcasted_iota(jnp.int32, (CHUNK, CHUNK), 1)
    incl = ri >= ci
    strict = ri > ci
    g_row = neg_a * _softplus(arow_ref[...] + dtb)
    g_col = neg_a * _softplus(acol_ref[...] + dtb)
    gcr_s[...] = jnp.dot(g_row, jnp.where(ri <= ci, 1.0, 0.0), precision=lax.Precision.HIGHEST,
                         preferred_element_type=F32)
    gc_col = jnp.dot(jnp.where(incl, 1.0, 0.0), g_col, precision=lax.Precision.HIGHEST,
                     preferred_element_type=F32)
    beta_col = _sigmoid(bcol_ref[...])
    chunk_lane = lax.broadcasted_iota(jnp.int32, (CHUNK, n_chunks), 1)

    def prep(c, carry):
        rows = pl.ds(pl.multiple_of(c * CHUNK, CHUNK), CHUNK)
        sel = chunk_lane == c
        bcol = jnp.sum(jnp.where(sel, beta_col, 0.0), axis=1, keepdims=True)
        gcol = jnp.sum(jnp.where(sel, gc_col, 0.0), axis=1, keepdims=True)
        grow = gcr_s[pl.ds(c, 1), :]
        glast = grow[:, CHUNK - 1:CHUNK]
        kc = k_s[rows, :]
        qc = q_s[rows, :]
        vc = v_s[rows, :]
        decay = jnp.exp(jnp.where(incl, gcol - grow, -jnp.inf))
        kb = kc * bcol
        vb = vc * bcol
        kcb = kc.astype(BF16)
        x = jnp.where(strict, -(_dot_nt(kb.astype(BF16), kcb) * decay), 0.0)
        zz, y = x, x
        for _ in range(5):
            yb = y.astype(BF16)
            y = _dot(yb, yb)
            zz = zz + y + _dot(zz.astype(BF16), y.astype(BF16))
        egc = jnp.exp(gcol)
        kexp = kb * egc
        rhs = jnp.concatenate([vb, kexp], axis=1)
        uw = rhs + _dot(zz.astype(BF16), rhs.astype(BF16))
        u_s[rows, :] = uw[:, :HEAD_DIM]
        w_s[rows, :] = uw[:, HEAD_DIM:].astype(BF16)
        attn = jnp.where(incl, _dot_nt(qc.astype(BF16), kcb) * decay, 0.0)
        at_s[rows, :] = attn.astype(BF16)
        qd_s[rows, :] = (qc * egc).astype(BF16)
        kt_s[rows, :] = (kc * jnp.exp(glast - gcol)).astype(BF16)
        return carry

    lax.fori_loop(0, n_chunks, prep, 0)

    ng = ng_ref[...]

    def scan(c, state):
        rows = pl.ds(pl.multiple_of(c * CHUNK, CHUNK), CHUNK)
        sb = state.astype(BF16)
        v_new = u_s[rows, :] - _dot(w_s[rows, :], sb)
        vnb = v_new.astype(BF16)
        o = _dot(qd_s[rows, :], sb) + _dot(at_s[rows, :], vnb)
        cd = jnp.exp(gcr_s[pl.ds(c, 1), CHUNK - 1:CHUNK])
        state = state * cd + _dot_tn(kt_s[rows, :], vnb)
        o_ref[rows, :] = (_rmsnorm(o, ng) * _silu(z_ref[rows, :])).astype(o_ref.dtype)
        return state

    lax.fori_loop(0, n_chunks, scan, jnp.zeros((HEAD_DIM, HEAD_DIM), F32))


def gated_deltanet(qkvz, ba_rows, ba_cols, conv_w, a_log, dt_bias, norm_g, layer, batch, seq, *, name):
    n_chunks = seq // CHUNK
    col = lambda part: pl.BlockSpec((seq, HEAD_DIM), lambda b, h: (b, part * GDN_HEADS + h))
    cw = lambda part: pl.BlockSpec((None, SHORT_CONV, HEAD_DIM),
                                   lambda b, h: (layer, 0, part * GDN_HEADS + h))
    rows = lambda part: pl.BlockSpec((None, None, n_chunks, CHUNK),
                                     lambda b, h: (b, part * GDN_HEADS + h, 0, 0))
    cols = lambda part: pl.BlockSpec((None, None, CHUNK, n_chunks),
                                     lambda b, h: (b, part * GDN_HEADS + h, 0, 0))
    smem = pl.BlockSpec(memory_space=pltpu.SMEM)
    return pl.pallas_call(
        functools.partial(_gdn_kernel, layer=layer, seq=seq),
        grid=(batch, GDN_HEADS),
        in_specs=[smem, smem, col(0), col(1), col(2), col(3), cw(0), cw(1), cw(2),
                  rows(0), rows(1), cols(0), cols(1),
                  pl.BlockSpec((None, 1, HEAD_DIM), lambda b, h: (layer, 0, 0))],
        out_specs=pl.BlockSpec((seq, HEAD_DIM), lambda b, h: (b, h)),
        out_shape=jax.ShapeDtypeStruct((batch * seq, GDN_WIDTH), BF16),
        scratch_shapes=[
            pltpu.VMEM((seq, HEAD_DIM), F32),
            pltpu.VMEM((seq, HEAD_DIM), F32),
            pltpu.VMEM((seq, HEAD_DIM), F32),
            pltpu.VMEM((seq, HEAD_DIM), F32),
            pltpu.VMEM((seq, HEAD_DIM), BF16),
            pltpu.VMEM((seq, HEAD_DIM), BF16),
            pltpu.VMEM((seq, HEAD_DIM), BF16),
            pltpu.VMEM((seq, CHUNK), BF16),
            pltpu.VMEM((n_chunks, CHUNK), F32),
        ],
        compiler_params=_params(2), name=name,
    )(a_log, dt_bias, qkvz, qkvz, qkvz, qkvz, conv_w, conv_w, conv_w,
      ba_rows, ba_rows, ba_cols, ba_cols, norm_g)


def _proj2_residual_kernel(x_ref, a1_ref, a2_ref, w1_ref, w2_ref, o_ref):
    o_ref[...] = x_ref[...] + _dot(a1_ref[...], w1_ref[...]) + _dot(a2_ref[...], w2_ref[...])


def proj2_residual(x, a1, a2, w, layer, *, tm, tn, name):
    m, n = x.shape
    k1, k2 = a1.shape[1], a2.shape[1]
    assert k1 == k2
    tm = min(tm, m)
    return pl.pallas_call(
        _proj2_residual_kernel,
        grid=(m // tm, n // tn),
        in_specs=[
            pl.BlockSpec((tm, tn), lambda i, j: (i, j)),
            pl.BlockSpec((tm, k1), lambda i, j: (i, 0)),
            pl.BlockSpec((tm, k2), lambda i, j: (i, 0)),
            pl.BlockSpec((None, k1, tn), lambda i, j: (layer, 0, j)),
            pl.BlockSpec((None, k2, tn), lambda i, j: (layer, 1, j)),
        ],
        out_specs=pl.BlockSpec((tm, tn), lambda i, j: (i, j)),
        out_shape=jax.ShapeDtypeStruct((m, n), F32),
        compiler_params=_params(2), name=name,
    )(x, a1, a2, w, w)


def _xattn_kernel(x_ref, g_ref, wq_ref, k_ref, v_ref, wo_ref, o_ref, h_s, a_s, *, tm):
    _rmsnorm_into(h_s, 0, x_ref, g_ref[...], tm)
    q = _dot(h_s[...], wq_ref[...]).astype(BF16)
    scale = HEAD_DIM ** -0.5
    for hd in range(X_HEADS):
        cols = slice(hd * HEAD_DIM, (hd + 1) * HEAD_DIM)
        s = _dot_nt(q[:, cols], k_ref[:, cols]) * scale
        p = jnp.exp(s - jnp.max(s, axis=-1, keepdims=True))
        denom = jnp.sum(p, axis=-1, keepdims=True)
        a_s[:, cols] = (_dot(p.astype(BF16), v_ref[:, cols]) / denom).astype(BF16)
    o_ref[...] = x_ref[...] + _dot(a_s[...], wo_ref[...])


def cross_attention_block(x, g, wq, kv, wo, layer, seq, n_mem, *, tm, name):
    m, d = x.shape
    tm = min(tm, seq)
    per_seq = seq // tm
    return pl.pallas_call(
        functools.partial(_xattn_kernel, tm=tm),
        grid=(m // tm,),
        in_specs=[
            pl.BlockSpec((tm, d), lambda i: (i, 0)),
            pl.BlockSpec((None, 1, d), lambda i: (layer, 0, 0)),
            pl.BlockSpec((None, d, X_WIDTH), lambda i: (layer, 0, 0)),
            pl.BlockSpec((n_mem, X_WIDTH), lambda i: (i // per_seq, 0)),
            pl.BlockSpec((n_mem, X_WIDTH), lambda i: (i // per_seq, 1)),
            pl.BlockSpec((None, X_WIDTH, d), lambda i: (layer, 0, 0)),
        ],
        out_specs=pl.BlockSpec((tm, d), lambda i: (i, 0)),
        out_shape=jax.ShapeDtypeStruct((m, d), F32),
        scratch_shapes=[pltpu.VMEM((tm, d), BF16), pltpu.VMEM((tm, X_WIDTH), BF16)],
        compiler_params=_params(1), name=name,
    )(x, g, wq, kv, kv, wo)


def _ffn_up_kernel(x_ref, halo_ref, g_ref, wg_ref, wu_ref, cg_ref, cu_ref, bg_ref, bu_ref,
                   o_ref, h_s, *, tm, tiles_per_seq):
    i = pl.program_id(0)

    @pl.when(pl.program_id(1) == 0)
    def _():
        keep = jnp.where(i % tiles_per_seq == 0, 0.0, 1.0)
        _rmsnorm_into(h_s, 0, halo_ref, g_ref[...], HALO_ROWS, scale=keep)
        _rmsnorm_into(h_s, HALO_ROWS, x_ref, g_ref[...], tm)

    def conv(y, cw_ref, b_ref):
        cw = cw_ref[...]
        out = cw[FFN_CONV - 1:FFN_CONV, :] * y[HALO_ROWS:, :]
        for tap in range(FFN_CONV - 1):
            shift = FFN_CONV - 1 - tap
            out = out + cw[tap:tap + 1, :] * y[HALO_ROWS - shift:HALO_ROWS - shift + tm, :]
        return out + b_ref[...]

    h = h_s[...]
    gate = conv(_dot(h, wg_ref[...]), cg_ref, bg_ref)
    up = conv(_dot(h, wu_ref[...]), cu_ref, bu_ref)
    o_ref[...] = (_silu(gate) * up).astype(o_ref.dtype)


def ffn_up(x, g, w_up, conv_w, conv_b, layer, seq, *, tm, tn, name):
    m, d = x.shape
    d_ff = w_up.shape[-1] // 2
    tm = min(tm, seq)
    assert d_ff % tn == 0 and seq % tm == 0 and tm % HALO_ROWS == 0
    nj = d_ff // tn
    halo_blocks = tm // HALO_ROWS
    return pl.pallas_call(
        functools.partial(_ffn_up_kernel, tm=tm, tiles_per_seq=seq // tm),
        grid=(m // tm, nj),
        in_specs=[
            pl.BlockSpec((tm, d), lambda i, j: (i, 0)),
            pl.BlockSpec((HALO_ROWS, d), lambda i, j: (jnp.maximum(i * halo_blocks - 1, 0), 0)),
            pl.BlockSpec((None, 1, d), lambda i, j: (layer, 0, 0)),
            pl.BlockSpec((None, d, tn), lambda i, j: (layer, 0, j)),
            pl.BlockSpec((None, d, tn), lambda i, j: (layer, 0, nj + j)),
            pl.BlockSpec((None, FFN_CONV, tn), lambda i, j: (layer, 0, j)),
            pl.BlockSpec((None, FFN_CONV, tn), lambda i, j: (layer, 0, nj + j)),
            pl.BlockSpec((None, 1, tn), lambda i, j: (layer, 0, j)),
            pl.BlockSpec((None, 1, tn), lambda i, j: (layer, 0, nj + j)),
        ],
        out_specs=pl.BlockSpec((tm, tn), lambda i, j: (i, j)),
        out_shape=jax.ShapeDtypeStruct((m, d_ff), BF16),
        scratch_shapes=[pltpu.VMEM((tm + HALO_ROWS, d), BF16)],
        compiler_params=_params(2), name=name,
    )(x, x, g, w_up, w_up, conv_w, conv_w, conv_b, conv_b)


def _matmul_residual_kernel(x_ref, a_ref, w_ref, o_ref):
    o_ref[...] = x_ref[...] + _dot(a_ref[...], w_ref[...])


def matmul_residual(x, a, w, layer, *, tm, tn, name):
    m, n = x.shape
    k = a.shape[1]
    tm = min(tm, m)
    return pl.pallas_call(
        _matmul_residual_kernel,
        grid=(m // tm, n // tn),
        in_specs=[
            pl.BlockSpec((tm, tn), lambda i, j: (i, j)),
            pl.BlockSpec((tm, k), lambda i, j: (i, 0)),
            pl.BlockSpec((None, k, tn), lambda i, j: (layer, 0, j)),
        ],
        out_specs=pl.BlockSpec((tm, tn), lambda i, j: (i, j)),
        out_shape=jax.ShapeDtypeStruct((m, n), F32),
        compiler_params=_params(2), name=name,
    )(x, a, w)


def _final_norm_kernel(x_ref, g_ref, o_ref, *, tm):
    _rmsnorm_into(o_ref, 0, x_ref, g_ref[...], tm)


def final_norm(x, g, *, tm, name):
    m, d = x.shape
    tm = min(tm, m)
    return pl.pallas_call(
        functools.partial(_final_norm_kernel, tm=tm),
        grid=(m // tm,),
        in_specs=[pl.BlockSpec((tm, d), lambda i: (i, 0)), pl.BlockSpec((1, d), lambda i: (0, 0))],
        out_specs=pl.BlockSpec((tm, d), lambda i: (i, 0)),
        out_shape=jax.ShapeDtypeStruct((m, d), F32),
        compiler_params=_params(1), name=name,
    )(x, g)


def kernel(x, mem, mix_norm, w_in, gdn_conv, gdn_a_log, gdn_dt_bias, gdn_norm, w_out, xattn_norm,
           mem_norm, w_xq, w_xkv, w_xo, ffn_norm, w_up, ffn_conv, ffn_conv_bias, w_down, final_norm_g):
    batch, seq, d_model = x.shape
    n_mem = mem.shape[1]
    depth = w_in.shape[0]
    n_chunks = seq // CHUNK
    n_main = 3 * SB_WIDTH + 4 * GDN_WIDTH
    n_gate = 2 * GDN_HEADS

    w_in_b = w_in[:, :, :n_main].astype(BF16)
    w_gate_b = jnp.pad(w_in[:, :, n_main:], ((0, 0), (0, 0), (0, HEAD_DIM - n_gate))).astype(BF16)
    w_out_b = w_out.astype(BF16)
    w_xq_b = w_xq.astype(BF16)
    w_xkv_b = w_xkv.astype(BF16)
    w_xo_b = w_xo.astype(BF16)
    w_up_b = w_up.astype(BF16)
    w_down_b = w_down.astype(BF16)
    row3 = lambda a: a.reshape(depth, 1, a.shape[-1])
    mix_g, xat_g, mem_g, ffn_g = row3(mix_norm), row3(xattn_norm), row3(mem_norm), row3(ffn_norm)
    gdn_g = row3(gdn_norm)
    conv_b = row3(ffn_conv_bias)

    xf = x.reshape(batch * seq, d_model)
    memf = mem.reshape(batch * n_mem, d_model)

    for l in range(depth):
        sb_qkv = norm_matmul(xf, mix_g, w_in_b, l, 0, 3 * SB_WIDTH, BF16, tm=1024, tn=1024,
                             name=f"inproj_sb_{l}")
        gdn_qkvz, gates = norm_matmul(xf, mix_g, w_in_b, l, 3 * SB_WIDTH, 4 * GDN_WIDTH, F32,
                                      tm=1024, tn=1024, side_w=w_gate_b, name=f"inproj_gdn_{l}")
        sb_out = sb_attention(sb_qkv, batch, seq, name=f"sb_attn_{l}")
        ba = gates[:, :n_gate].reshape(batch, seq, n_gate).transpose(0, 2, 1)
        ba_rows = ba.reshape(batch, n_gate, n_chunks, CHUNK)
        ba_cols = ba_rows.transpose(0, 1, 3, 2)
        gdn_out = gated_deltanet(gdn_qkvz, ba_rows, ba_cols, gdn_conv, gdn_a_log, gdn_dt_bias, gdn_g,
                                 l, batch, seq, name=f"gdn_{l}")
        xf = proj2_residual(xf, sb_out, gdn_out, w_out_b, l, tm=1024, tn=1024, name=f"outproj_{l}")

        kv = norm_matmul(memf, mem_g, w_xkv_b, l, 0, 2 * X_WIDTH, BF16, tm=512, tn=512,
                         name=f"mem_kv_{l}")
        xf = cross_attention_block(xf, xat_g, w_xq_b, kv, w_xo_b, l, seq, n_mem, tm=512,
                                   name=f"xattn_{l}")

        act = ffn_up(xf, ffn_g, w_up_b, ffn_conv, conv_b, l, seq, tm=1024, tn=512, name=f"ffn_up_{l}")
        xf = matmul_residual(xf, act, w_down_b, l, tm=1024, tn=512, name=f"ffn_down_{l}")

    out = final_norm(xf, final_norm_g.reshape(1, d_model), tm=512, name="final_norm")
    return out.reshape(batch, seq, d_model)
```

```python
import functools
import math

import jax
import jax.numpy as jnp
from jax import lax
from jax.experimental import pallas as pl
from jax.experimental.pallas import tpu as pltpu

F32 = jnp.float32
BF16 = jnp.bfloat16

HEAD_DIM = 128
SB_HEADS = 8
GDN_HEADS = 8
SB_WIDTH = SB_HEADS * HEAD_DIM
GDN_WIDTH = GDN_HEADS * HEAD_DIM
SHORT_CONV = 4
CHUNK = 64
GDN_SUPER = 256
GDN_GROUP = 2
X_HEADS = 4
X_WIDTH = X_HEADS * HEAD_DIM
FFN_CONV = 3
EPS = 1e-6

V7X_VMEM_BYTES = 64 * 1024 * 1024
VMEM_LIMIT_BYTES = V7X_VMEM_BYTES - 8 * 1024 * 1024
NORM_ROWS = 64
HALO_ROWS = 16


def _params(n_axes):
    return pltpu.CompilerParams(
        dimension_semantics=("arbitrary",) * n_axes, vmem_limit_bytes=VMEM_LIMIT_BYTES)


def _rmsnorm(x, g):
    ms = jnp.mean(x * x, axis=-1, keepdims=True)
    return x * lax.rsqrt(ms + EPS) * g


def _rmsnorm_into(dst_ref, dst_row0, x_ref, g, n_rows, scale=None):
    step = min(NORM_ROWS, n_rows)
    dst_align = math.gcd(step, dst_row0) if dst_row0 else step

    def body(r, carry):
        src = pl.ds(pl.multiple_of(r * step, step), step)
        dst = pl.ds(pl.multiple_of(dst_row0 + r * step, dst_align), step)
        y = _rmsnorm(x_ref[src, :], g)
        if scale is not None:
            y = y * scale
        dst_ref[dst, :] = y.astype(dst_ref.dtype)
        return carry

    lax.fori_loop(0, n_rows // step, body, 0)


def _sigmoid(x):
    return 1.0 / (1.0 + jnp.exp(-x))


def _silu(x):
    return x * _sigmoid(x)


def _softplus(x):
    return jnp.maximum(x, 0.0) + jnp.log1p(jnp.exp(-jnp.abs(x)))


def _dot(a, b):
    return jnp.dot(a, b, preferred_element_type=F32)


def _dot_nt(a, b):
    return lax.dot_general(a, b, (((1,), (1,)), ((), ())), preferred_element_type=F32)


def _dot_tn(a, b):
    return lax.dot_general(a, b, (((0,), (0,)), ((), ())), preferred_element_type=F32)


def _norm_matmul_kernel(x_ref, g_ref, w_ref, *rest, tm, has_side):
    if has_side:
        ws_ref, o_ref, os_ref, h_ref = rest
    else:
        o_ref, h_ref = rest

    @pl.when(pl.program_id(1) == 0)
    def _():
        _rmsnorm_into(h_ref, 0, x_ref, g_ref[...], tm)
        if has_side:
            os_ref[...] = _dot(h_ref[...], ws_ref[...]).astype(os_ref.dtype)

    o_ref[...] = _dot(h_ref[...], w_ref[...]).astype(o_ref.dtype)


def norm_matmul(x, g, w, layer, col0, n_cols, out_dtype, *, tm, tn, side_w=None, name):
    m, k = x.shape
    tm = min(tm, m)
    assert m % tm == 0 and n_cols % tn == 0 and col0 % tn == 0
    off = col0 // tn
    in_specs = [
        pl.BlockSpec((tm, k), lambda i, j: (i, 0)),
        pl.BlockSpec((None, 1, k), lambda i, j: (layer, 0, 0)),
        pl.BlockSpec((None, k, tn), lambda i, j: (layer, 0, j + off)),
    ]
    out_shape = [jax.ShapeDtypeStruct((m, n_cols), out_dtype)]
    out_specs = [pl.BlockSpec((tm, tn), lambda i, j: (i, j))]
    args = [x, g, w]
    if side_w is not None:
        ns = side_w.shape[-1]
        in_specs.append(pl.BlockSpec((None, k, ns), lambda i, j: (layer, 0, 0)))
        out_shape.append(jax.ShapeDtypeStruct((m, ns), F32))
        out_specs.append(pl.BlockSpec((tm, ns), lambda i, j: (i, 0)))
        args.append(side_w)
    out = pl.pallas_call(
        functools.partial(_norm_matmul_kernel, tm=tm, has_side=side_w is not None),
        grid=(m // tm, n_cols // tn),
        in_specs=in_specs, out_specs=out_specs, out_shape=out_shape,
        scratch_shapes=[pltpu.VMEM((tm, k), BF16)],
        compiler_params=_params(2), name=name,
    )(*args)
    return out if side_w is not None else out[0]


def _sb_attn_kernel(q_ref, k_ref, v_ref, o_ref, acc_s, carry_s, *, seq, tile, sub):
    scale = HEAD_DIM ** -0.5
    n_sub = tile // sub
    r = lax.broadcasted_iota(jnp.int32, (sub, sub), 0)
    c = lax.broadcasted_iota(jnp.int32, (sub, sub), 1)
    suffix = jnp.where(r > c, 1.0, 0.0).astype(BF16)
    row = lax.broadcasted_iota(jnp.int32, (tile, tile), 0)
    col = lax.broadcasted_iota(jnp.int32, (tile, tile), 1)
    causal = col < row

    def block_terms(q, kb, carry, masked):
        rows = pl.ds(pl.multiple_of(kb * tile, tile), tile)
        z = _dot_nt(q, k_ref[rows, :]) * scale
        log_beta = jnp.minimum(z, 0.0) - jnp.log(1.0 + jnp.exp(-jnp.abs(z)))
        log_stay = log_beta - z
        if masked:
            log_stay = jnp.where(causal, log_stay, 0.0)
        laters = [None] * n_sub
        run = carry
        for s in reversed(range(n_sub)):
            ls = log_stay[:, s * sub:(s + 1) * sub]
            hi = ls.astype(BF16)
            lo = (ls - hi.astype(F32)).astype(BF16)
            cum = _dot(hi, suffix) + _dot(lo, suffix)
            laters[s] = cum if run is None else cum + run
            tot = jnp.sum(ls, axis=-1, keepdims=True)
            run = tot if run is None else run + tot
        w = jnp.exp(log_beta + jnp.concatenate(laters, axis=1))
        if masked:
            w = jnp.where(causal, w, 0.0)
        return _dot(w.astype(BF16), v_ref[rows, :]), run

    def q_block(qi, c):
        qrows = pl.ds(pl.multiple_of(qi * tile, tile), tile)
        q = q_ref[qrows, :]
        acc_s[...], carry_s[...] = block_terms(q, qi, None, True)

        def k_step(s, c2):
            pv, run = block_terms(q, qi - 1 - s, carry_s[...], False)
            acc_s[...] += pv
            carry_s[...] = run
            return c2

        lax.fori_loop(0, qi, k_step, 0)
        o_ref[qrows, :] = acc_s[...].astype(o_ref.dtype)
        return c

    lax.fori_loop(0, seq // tile, q_block, 0)


def sb_attention(qkv, batch, seq, *, tile=512, sub=256, name):
    tile = min(tile, seq)
    sub = min(sub, tile)
    spec = lambda part: pl.BlockSpec((seq, HEAD_DIM), lambda b, h: (b, part * SB_HEADS + h))
    return pl.pallas_call(
        functools.partial(_sb_attn_kernel, seq=seq, tile=tile, sub=sub),
        grid=(batch, SB_HEADS),
        in_specs=[spec(0), spec(1), spec(2)],
        out_specs=pl.BlockSpec((seq, HEAD_DIM), lambda b, h: (b, h)),
        out_shape=jax.ShapeDtypeStruct((batch * seq, SB_WIDTH), BF16),
        scratch_shapes=[pltpu.VMEM((tile, HEAD_DIM), F32), pltpu.VMEM((tile, 1), F32)],
        compiler_params=_params(2), name=name,
    )(qkv, qkv, qkv)


def _gdn_kernel(alog_ref, dtb_ref, xq_ref, xk_ref, xv_ref, z_ref, cwq_ref, cwk_ref, cwv_ref,
                brow_ref, arow_ref, bcol_ref, acol_ref, ng_ref, o_ref,
                q_s, k_s, v_s, u_s, w_s, qd_s, kt_s, at_s, gcr_s, gcol_s, *, layer, seq):
    sup = min(GDN_SUPER, seq)
    per = sup // CHUNK
    n_sup = seq // sup
    group = pl.program_id(1)
    conv_rows = min(256, seq)
    lanes = lambda hh: slice(hh * HEAD_DIM, (hh + 1) * HEAD_DIM)

    ri = lax.broadcasted_iota(jnp.int32, (sup, sup), 0)
    ci = lax.broadcasted_iota(jnp.int32, (sup, sup), 1)
    same = (ri // CHUNK) == (ci // CHUNK)
    in_lower = jnp.logical_and(same, ri >= ci)
    neg_inf_mask = jnp.where(in_lower, 0.0, -jnp.inf)
    neg_strict = jnp.where(jnp.logical_and(same, ri > ci), -1.0, 0.0)
    tri_lower = jnp.where(in_lower, 1.0, 0.0)
    tri_upper = jnp.where(jnp.logical_and(same, ri <= ci), 1.0, 0.0)
    block_ones = jnp.where(same, 1.0, 0.0)
    sup_lane = lax.broadcasted_iota(jnp.int32, (sup, n_sup), 1)

    def l2norm(y):
        return y * lax.rsqrt(jnp.sum(y * y, axis=-1, keepdims=True) + EPS)

    def conv_phase(x_ref, cw_ref, dst_ref, hh, post):
        cw = cw_ref[:, lanes(hh)]

        def body(r, prev):
            rows = pl.ds(pl.multiple_of(r * conv_rows, conv_rows), conv_rows)
            xb = x_ref[rows, lanes(hh)]
            xx = jnp.concatenate([prev, xb], axis=0)
            y = cw[SHORT_CONV - 1:SHORT_CONV, :] * xx[8:, :]
            for tap in range(SHORT_CONV - 1):
                shift = SHORT_CONV - 1 - tap
                y = y + cw[tap:tap + 1, :] * xx[8 - shift:8 - shift + conv_rows, :]
            dst_ref[hh, rows, :] = post(_silu(y))
            return xb[conv_rows - 8:, :]

        lax.fori_loop(0, seq // conv_rows, body, jnp.zeros((8, HEAD_DIM), F32))

    hdot = functools.partial(jnp.dot, precision=lax.Precision.HIGHEST, preferred_element_type=F32)
    for hh in range(GDN_GROUP):
        conv_phase(xq_ref, cwq_ref, q_s, hh, lambda y: l2norm(y) * (HEAD_DIM ** -0.5))
        conv_phase(xk_ref, cwk_ref, k_s, hh, l2norm)
        conv_phase(xv_ref, cwv_ref, v_s, hh, lambda y: y)
        head = group * GDN_GROUP + hh
        neg_a = -jnp.exp(jnp.full((1, 1), alog_ref[layer, head], F32))
        dtb = dtb_ref[layer, head]
        g_row = neg_a * _softplus(arow_ref[hh] + dtb)
        g_col = neg_a * _softplus(acol_ref[hh] + dtb)
        gcr_s[hh] = hdot(g_row, tri_upper)
        gcol_s[hh, 0] = _sigmoid(bcol_ref[hh])
        gcol_s[hh, 1] = hdot(tri_lower, g_col)
        gcol_s[hh, 2] = hdot(block_ones, g_col)

    def prep_one(hh, sc):
        rows = pl.ds(pl.multiple_of(sc * sup, sup), sup)
        sel = sup_lane == sc
        pick = lambda kind: jnp.sum(jnp.where(sel, gcol_s[hh, kind], 0.0), axis=1, keepdims=True)
        bcol, gcol, glast = pick(0), pick(1), pick(2)
        grow = gcr_s[hh, pl.ds(sc, 1), :]
        kc = k_s[hh, rows, :]
        qc = q_s[hh, rows, :]
        decay = jnp.exp((gcol - grow) + neg_inf_mask)
        kb = kc * bcol
        vb = v_s[hh, rows, :] * bcol
        kcb = kc.astype(BF16)
        x = _dot_nt(kb.astype(BF16), kcb) * decay * neg_strict
        zz, y = x, x
        n_sq = max(CHUNK.bit_length() - 2, 0)
        for _ in range(n_sq):
            yb = y.astype(BF16)
            y = _dot(yb, yb)
            zz = zz + y + _dot(zz.astype(BF16), y.astype(BF16))
        egc = jnp.exp(gcol)
        rhs = jnp.concatenate([vb, kb * egc], axis=1)
        uw = rhs + _dot(zz.astype(BF16), rhs.astype(BF16))
        u_s[hh, rows, :] = uw[:, :HEAD_DIM]
        w_s[hh, rows, :] = uw[:, HEAD_DIM:].astype(BF16)
        attn = (_dot_nt(qc.astype(BF16), kcb) * decay).astype(BF16)
        for p in range(per):
            blk = slice(p * CHUNK, (p + 1) * CHUNK)
            at_s[hh, pl.ds(pl.multiple_of(sc * sup + p * CHUNK, CHUNK), CHUNK), :] = attn[blk, blk]
        qd_s[hh, rows, :] = (qc * egc).astype(BF16)
        kt_s[hh, rows, :] = (kc * jnp.exp(glast - gcol)).astype(BF16)

    def prep(sc, carry):
        for hh in range(GDN_GROUP):
            prep_one(hh, sc)
        return carry

    lax.fori_loop(0, n_sup, prep, 0)

    ng = ng_ref[...]

    def scan(sc, states):
        states = list(states)
        chunk_decay = [jnp.exp(gcr_s[hh, pl.ds(sc, 1), :]) for hh in range(GDN_GROUP)]
        for p in range(per):
            rows = pl.ds(pl.multiple_of(sc * sup + p * CHUNK, CHUNK), CHUNK)
            for hh in range(GDN_GROUP):
                state = states[hh]
                sb = state.astype(BF16)
                v_new = u_s[hh, rows, :] - _dot(w_s[hh, rows, :], sb)
                vnb = v_new.astype(BF16)
                o = _dot(qd_s[hh, rows, :], sb) + _dot(at_s[hh, rows, :], vnb)
                last = p * CHUNK + CHUNK - 1
                cd = chunk_decay[hh][:, last:last + 1]
                states[hh] = state * cd + _dot_tn(kt_s[hh, rows, :], vnb)
                o_ref[rows, lanes(hh)] = (_rmsnorm(o, ng) * _silu(z_ref[rows, lanes(hh)])).astype(o_ref.dtype)
        return tuple(states)

    lax.fori_loop(0, n_sup, scan, tuple(jnp.zeros((HEAD_DIM, HEAD_DIM), F32) for _ in range(GDN_GROUP)))


def gated_deltanet(qkvz, ba_rows, ba_cols, conv_w, a_log, dt_bias, norm_g, layer, batch, seq, *, name):
    sup = min(GDN_SUPER, seq)
    n_sup = seq // sup
    n_groups = GDN_HEADS // GDN_GROUP
    width = GDN_GROUP * HEAD_DIM
    col = lambda part: pl.BlockSpec((seq, width), lambda b, g: (b, part * n_groups + g))
    cw = lambda part: pl.BlockSpec((None, SHORT_CONV, width), lambda b, g: (layer, 0, part * n_groups + g))
    rows = lambda part: pl.BlockSpec((None, GDN_GROUP, n_sup, sup), lambda b, g: (b, part * n_groups + g, 0, 0))
    cols = lambda part: pl.BlockSpec((None, GDN_GROUP, sup, n_sup), lambda b, g: (b, part * n_groups + g, 0, 0))
    smem = pl.BlockSpec(memory_space=pltpu.SMEM)
    per_head = lambda shape, dt: pltpu.VMEM((GDN_GROUP,) + shape, dt)
    return pl.pallas_call(
        functools.partial(_gdn_kernel, layer=layer, seq=seq),
        grid=(batch, n_groups),
        in_specs=[smem, smem, col(0), col(1), col(2), col(3), cw(0), cw(1), cw(2),
                  rows(0), rows(1), cols(0), cols(1),
                  pl.BlockSpec((None, 1, HEAD_DIM), lambda b, g: (layer, 0, 0))],
        out_specs=pl.BlockSpec((seq, width), lambda b, g: (b, g)),
        out_shape=jax.ShapeDtypeStruct((batch * seq, GDN_WIDTH), BF16),
        scratch_shapes=[
            per_head((seq, HEAD_DIM), F32),
            per_head((seq, HEAD_DIM), F32),
            per_head((seq, HEAD_DIM), F32),
            per_head((seq, HEAD_DIM), F32),
            per_head((seq, HEAD_DIM), BF16),
            per_head((seq, HEAD_DIM), BF16),
            per_head((seq, HEAD_DIM), BF16),
            per_head((seq, CHUNK), BF16),
            per_head((n_sup, sup), F32),
            per_head((3, sup, n_sup), F32),
        ],
        compiler_params=_params(2), name=name,
    )(a_log, dt_bias, qkvz, qkvz, qkvz, qkvz, conv_w, conv_w, conv_w,
      ba_rows, ba_rows, ba_cols, ba_cols, norm_g)


def _proj2_residual_kernel(x_ref, a1_ref, a2_ref, w1_ref, w2_ref, o_ref):
    o_ref[...] = x_ref[...] + _dot(a1_ref[...], w1_ref[...]) + _dot(a2_ref[...], w2_ref[...])


def proj2_residual(x, a1, a2, w, layer, *, tm, tn, name):
    m, n = x.shape
    k1, k2 = a1.shape[1], a2.shape[1]
    assert k1 == k2
    tm = min(tm, m)
    return pl.pallas_call(
        _proj2_residual_kernel,
        grid=(m // tm, n // tn),
        in_specs=[
            pl.BlockSpec((tm, tn), lambda i, j: (i, j)),
            pl.BlockSpec((tm, k1), lambda i, j: (i, 0)),
            pl.BlockSpec((tm, k2), lambda i, j: (i, 0)),
            pl.BlockSpec((None, k1, tn), lambda i, j: (layer, 0, j)),
            pl.BlockSpec((None, k2, tn), lambda i, j: (layer, 1, j)),
        ],
        out_specs=pl.BlockSpec((tm, tn), lambda i, j: (i, j)),
        out_shape=jax.ShapeDtypeStruct((m, n), F32),
        compiler_params=_params(2), name=name,
    )(x, a1, a2, w, w)


def _xattn_kernel(x_ref, g_ref, wq_ref, k_ref, v_ref, wo_ref, o_ref, h_s, a_s, *, tm):
    _rmsnorm_into(h_s, 0, x_ref, g_ref[...], tm)
    q = _dot(h_s[...], wq_ref[...]).astype(BF16)
    scale = HEAD_DIM ** -0.5
    for hd in range(X_HEADS):
        cols = slice(hd * HEAD_DIM, (hd + 1) * HEAD_DIM)
        s = _dot_nt(q[:, cols], k_ref[:, cols]) * scale
        p = jnp.exp(s - jnp.max(s, axis=-1, keepdims=True))
        denom = jnp.sum(p, axis=-1, keepdims=True)
        a_s[:, cols] = (_dot(p.astype(BF16), v_ref[:, cols]) / denom).astype(BF16)
    o_ref[...] = x_ref[...] + _dot(a_s[...], wo_ref[...])


def cross_attention_block(x, g, wq, kv, wo, layer, seq, n_mem, *, tm, name):
    m, d = x.shape
    tm = min(tm, seq)
    per_seq = seq // tm
    return pl.pallas_call(
        functools.partial(_xattn_kernel, tm=tm),
        grid=(m // tm,),
        in_specs=[
            pl.BlockSpec((tm, d), lambda i: (i, 0)),
            pl.BlockSpec((None, 1, d), lambda i: (layer, 0, 0)),
            pl.BlockSpec((None, d, X_WIDTH), lambda i: (layer, 0, 0)),
            pl.BlockSpec((n_mem, X_WIDTH), lambda i: (i // per_seq, 0)),
            pl.BlockSpec((n_mem, X_WIDTH), lambda i: (i // per_seq, 1)),
            pl.BlockSpec((None, X_WIDTH, d), lambda i: (layer, 0, 0)),
        ],
        out_specs=pl.BlockSpec((tm, d), lambda i: (i, 0)),
        out_shape=jax.ShapeDtypeStruct((m, d), F32),
        scratch_shapes=[pltpu.VMEM((tm, d), BF16), pltpu.VMEM((tm, X_WIDTH), BF16)],
        compiler_params=_params(1), name=name,
    )(x, g, wq, kv, kv, wo)


def _ffn_up_kernel(x_ref, halo_ref, g_ref, wg_ref, wu_ref, cg_ref, cu_ref, bg_ref, bu_ref,
                   o_ref, h_s, *, tm, tiles_per_seq):
    i = pl.program_id(0)

    @pl.when(pl.program_id(1) == 0)
    def _():
        keep = jnp.where(i % tiles_per_seq == 0, 0.0, 1.0)
        _rmsnorm_into(h_s, 0, halo_ref, g_ref[...], HALO_ROWS, scale=keep)
        _rmsnorm_into(h_s, HALO_ROWS, x_ref, g_ref[...], tm)

    def conv(y, cw_ref, b_ref):
        cw = cw_ref[...]
        out = cw[FFN_CONV - 1:FFN_CONV, :] * y[HALO_ROWS:, :]
        for tap in range(FFN_CONV - 1):
            shift = FFN_CONV - 1 - tap
            out = out + cw[tap:tap + 1, :] * y[HALO_ROWS - shift:HALO_ROWS - shift + tm, :]
        return out + b_ref[...]

    h = h_s[...]
    gate = conv(_dot(h, wg_ref[...]), cg_ref, bg_ref)
    up = conv(_dot(h, wu_ref[...]), cu_ref, bu_ref)
    o_ref[...] = (_silu(gate) * up).astype(o_ref.dtype)


def ffn_up(x, g, w_up, conv_w, conv_b, layer, seq, *, tm, tn, name):
    m, d = x.shape
    d_ff = w_up.shape[-1] // 2
    tm = min(tm, seq)
    assert d_ff % tn == 0 and seq % tm == 0 and tm % HALO_ROWS == 0
    nj = d_ff // tn
    halo_blocks = tm // HALO_ROWS
    return pl.pallas_call(
        functools.partial(_ffn_up_kernel, tm=tm, tiles_per_seq=seq // tm),
        grid=(m // tm, nj),
        in_specs=[
            pl.BlockSpec((tm, d), lambda i, j: (i, 0)),
            pl.BlockSpec((HALO_ROWS, d), lambda i, j: (jnp.maximum(i * halo_blocks - 1, 0), 0)),
            pl.BlockSpec((None, 1, d), lambda i, j: (layer, 0, 0)),
            pl.BlockSpec((None, d, tn), lambda i, j: (layer, 0, j)),
            pl.BlockSpec((None, d, tn), lambda i, j: (layer, 0, nj + j)),
            pl.BlockSpec((None, FFN_CONV, tn), lambda i, j: (layer, 0, j)),
            pl.BlockSpec((None, FFN_CONV, tn), lambda i, j: (layer, 0, nj + j)),
            pl.BlockSpec((None, 1, tn), lambda i, j: (layer, 0, j)),
            pl.BlockSpec((None, 1, tn), lambda i, j: (layer, 0, nj + j)),
        ],
        out_specs=pl.BlockSpec((tm, tn), lambda i, j: (i, j)),
        out_shape=jax.ShapeDtypeStruct((m, d_ff), BF16),
        scratch_shapes=[pltpu.VMEM((tm + HALO_ROWS, d), BF16)],
        compiler_params=_params(2), name=name,
    )(x, x, g, w_up, w_up, conv_w, conv_w, conv_b, conv_b)


def _matmul_residual_kernel(x_ref, a_ref, w_ref, o_ref):
    o_ref[...] = x_ref[...] + _dot(a_ref[...], w_ref[...])


def matmul_residual(x, a, w, layer, *, tm, tn, name):
    m, n = x.shape
    k = a.shape[1]
    tm = min(tm, m)
    return pl.pallas_call(
        _matmul_residual_kernel,
        grid=(m // tm, n // tn),
        in_specs=[
            pl.BlockSpec((tm, tn), lambda i, j: (i, j)),
            pl.BlockSpec((tm, k), lambda i, j: (i, 0)),
            pl.BlockSpec((None, k, tn), lambda i, j: (layer, 0, j)),
        ],
        out_specs=pl.BlockSpec((tm, tn), lambda i, j: (i, j)),
        out_shape=jax.ShapeDtypeStruct((m, n), F32),
        compiler_params=_params(2), name=name,
    )(x, a, w)


def _final_norm_kernel(x_ref, g_ref, o_ref, *, tm):
    _rmsnorm_into(o_ref, 0, x_ref, g_ref[...], tm)


def final_norm(x, g, *, tm, name):
    m, d = x.shape
    tm = min(tm, m)
    return pl.pallas_call(
        functools.partial(_final_norm_kernel, tm=tm),
        grid=(m // tm,),
        in_specs=[pl.BlockSpec((tm, d), lambda i: (i, 0)), pl.BlockSpec((1, d), lambda i: (0, 0))],
        out_specs=pl.BlockSpec((tm, d), lambda i: (i, 0)),
        out_shape=jax.ShapeDtypeStruct((m, d), F32),
        compiler_params=_params(1), name=name,
    )(x, g)


def kernel(x, mem, mix_norm, w_in, gdn_conv, gdn_a_log, gdn_dt_bias, gdn_norm, w_out, xattn_norm,
           mem_norm, w_xq, w_xkv, w_xo, ffn_norm, w_up, ffn_conv, ffn_conv_bias, w_down, final_norm_g):
    batch, seq, d_model = x.shape
    n_mem = mem.shape[1]
    depth = w_in.shape[0]
    sup = min(GDN_SUPER, seq)
    n_main = 3 * SB_WIDTH + 4 * GDN_WIDTH
    n_gate = 2 * GDN_HEADS

    w_in_b = w_in[:, :, :n_main].astype(BF16)
    w_gate_b = jnp.pad(w_in[:, :, n_main:], ((0, 0), (0, 0), (0, HEAD_DIM - n_gate))).astype(BF16)
    w_out_b = w_out.astype(BF16)
    w_xq_b = w_xq.astype(BF16)
    w_xkv_b = w_xkv.astype(BF16)
    w_xo_b = w_xo.astype(BF16)
    w_up_b = w_up.astype(BF16)
    w_down_b = w_down.astype(BF16)
    row3 = lambda a: a.reshape(depth, 1, a.shape[-1])
    mix_g, xat_g, mem_g, ffn_g = row3(mix_norm), row3(xattn_norm), row3(mem_norm), row3(ffn_norm)
    gdn_g = row3(gdn_norm)
    conv_b = row3(ffn_conv_bias)

    xf = x.reshape(batch * seq, d_model)
    memf = mem.reshape(batch * n_mem, d_model)

    for l in range(depth):
        sb_qkv = norm_matmul(xf, mix_g, w_in_b, l, 0, 3 * SB_WIDTH, BF16, tm=1024, tn=1024,
                             name=f"inproj_sb_{l}")
        gdn_qkvz, gates = norm_matmul(xf, mix_g, w_in_b, l, 3 * SB_WIDTH, 4 * GDN_WIDTH, F32,
                                      tm=1024, tn=1024, side_w=w_gate_b, name=f"inproj_gdn_{l}")
        sb_out = sb_attention(sb_qkv, batch, seq, name=f"sb_attn_{l}")
        ba = gates[:, :n_gate].reshape(batch, seq, n_gate).transpose(0, 2, 1)
        ba_rows = ba.reshape(batch, n_gate, seq // sup, sup)
        ba_cols = ba_rows.transpose(0, 1, 3, 2)
        gdn_out = gated_deltanet(gdn_qkvz, ba_rows, ba_cols, gdn_conv, gdn_a_log, gdn_dt_bias, gdn_g,
                                 l, batch, seq, name=f"gdn_{l}")
        xf = proj2_residual(xf, sb_out, gdn_out, w_out_b, l, tm=1024, tn=1024, name=f"outproj_{l}")

        kv = norm_matmul(memf, mem_g, w_xkv_b, l, 0, 2 * X_WIDTH, BF16, tm=512, tn=512,
                         name=f"mem_kv_{l}")
        xf = cross_attention_block(xf, xat_g, w_xq_b, kv, w_xo_b, l, seq, n_mem, tm=512,
                                   name=f"xattn_{l}")

        act = ffn_up(xf, ffn_g, w_up_b, ffn_conv, conv_b, l, seq, tm=1024, tn=512, name=f"ffn_up_{l}")
        xf = matmul_residual(xf, act, w_down_b, l, tm=1024, tn=512, name=f"ffn_down_{l}")

    out = final_norm(xf, final_norm_g.reshape(1, d_model), tm=512, name="final_norm")
    return out.reshape(batch, seq, d_model)
```

```python
import functools
import math

import jax
import jax.numpy as jnp
from jax import lax
from jax.experimental import pallas as pl
from jax.experimental.pallas import tpu as pltpu

F32 = jnp.float32
BF16 = jnp.bfloat16

HEAD_DIM = 128
SB_HEADS = 8
SB_GROUP = 2
GDN_HEADS = 8
SB_WIDTH = SB_HEADS * HEAD_DIM
GDN_WIDTH = GDN_HEADS * HEAD_DIM
SHORT_CONV = 4
CHUNK = 64
GDN_SUPER = 256
GDN_GROUP = 4
X_HEADS = 4
X_WIDTH = X_HEADS * HEAD_DIM
FFN_CONV = 3
EPS = 1e-6

V7X_VMEM_BYTES = 64 * 1024 * 1024
VMEM_LIMIT_BYTES = V7X_VMEM_BYTES - 8 * 1024 * 1024
NORM_ROWS = 64
HALO_ROWS = 16


def _params(n_axes):
    return pltpu.CompilerParams(
        dimension_semantics=("arbitrary",) * n_axes, vmem_limit_bytes=VMEM_LIMIT_BYTES)


def _rmsnorm(x, g):
    ms = jnp.mean(x * x, axis=-1, keepdims=True)
    return x * lax.rsqrt(ms + EPS) * g


def _rmsnorm_into(dst_ref, dst_row0, x_ref, g, n_rows, scale=None):
    step = min(NORM_ROWS, n_rows)
    dst_align = math.gcd(step, dst_row0) if dst_row0 else step

    def body(r, carry):
        src = pl.ds(pl.multiple_of(r * step, step), step)
        dst = pl.ds(pl.multiple_of(dst_row0 + r * step, dst_align), step)
        y = _rmsnorm(x_ref[src, :], g)
        if scale is not None:
            y = y * scale
        dst_ref[dst, :] = y.astype(dst_ref.dtype)
        return carry

    lax.fori_loop(0, n_rows // step, body, 0)


def _aligned(start, multiple):
    return start if isinstance(start, int) else pl.multiple_of(start, multiple)


def _interleave(*gens):
    live = list(gens)
    while live:
        for g in list(live):
            if next(g, live) is live:
                live.remove(g)


def _sigmoid(x):
    return 1.0 / (1.0 + jnp.exp(-x))


def _silu(x):
    return x * _sigmoid(x)


def _softplus(x):
    return jnp.maximum(x, 0.0) + jnp.log1p(jnp.exp(-jnp.abs(x)))


def _dot(a, b):
    return jnp.dot(a, b, preferred_element_type=F32)


def _dot_nt(a, b):
    return lax.dot_general(a, b, (((1,), (1,)), ((), ())), preferred_element_type=F32)


def _dot_tn(a, b):
    return lax.dot_general(a, b, (((0,), (0,)), ((), ())), preferred_element_type=F32)


def _norm_matmul_kernel(x_ref, g_ref, w_ref, *rest, tm, has_side):
    if has_side:
        ws_ref, o_ref, os_ref, h_ref = rest
    else:
        o_ref, h_ref = rest

    @pl.when(pl.program_id(1) == 0)
    def _():
        _rmsnorm_into(h_ref, 0, x_ref, g_ref[...], tm)
        if has_side:
            os_ref[...] = _dot(h_ref[...], ws_ref[...]).astype(os_ref.dtype)

    o_ref[...] = _dot(h_ref[...], w_ref[...]).astype(o_ref.dtype)


def norm_matmul(x, g, w, layer, col0, n_cols, out_dtype, *, tm, tn, side_w=None, name):
    m, k = x.shape
    tm = min(tm, m)
    assert m % tm == 0 and n_cols % tn == 0 and col0 % tn == 0
    off = col0 // tn
    in_specs = [
        pl.BlockSpec((tm, k), lambda i, j: (i, 0)),
        pl.BlockSpec((None, 1, k), lambda i, j: (layer, 0, 0)),
        pl.BlockSpec((None, k, tn), lambda i, j: (layer, 0, j + off)),
    ]
    out_shape = [jax.ShapeDtypeStruct((m, n_cols), out_dtype)]
    out_specs = [pl.BlockSpec((tm, tn), lambda i, j: (i, j))]
    args = [x, g, w]
    if side_w is not None:
        ns = side_w.shape[-1]
        in_specs.append(pl.BlockSpec((None, k, ns), lambda i, j: (layer, 0, 0)))
        out_shape.append(jax.ShapeDtypeStruct((m, ns), F32))
        out_specs.append(pl.BlockSpec((tm, ns), lambda i, j: (i, 0)))
        args.append(side_w)
    out = pl.pallas_call(
        functools.partial(_norm_matmul_kernel, tm=tm, has_side=side_w is not None),
        grid=(m // tm, n_cols // tn),
        in_specs=in_specs, out_specs=out_specs, out_shape=out_shape,
        scratch_shapes=[pltpu.VMEM((tm, k), BF16)],
        compiler_params=_params(2), name=name,
    )(*args)
    return out if side_w is not None else out[0]


def _sb_attn_kernel(q_ref, k_ref, v_ref, o_ref, acc_s, carry_s, *, seq, tile, sub):
    scale = HEAD_DIM ** -0.5
    n_sub = tile // sub
    r = lax.broadcasted_iota(jnp.int32, (sub, sub), 0)
    c = lax.broadcasted_iota(jnp.int32, (sub, sub), 1)
    suffix = jnp.where(r > c, 1.0, 0.0).astype(BF16)
    row = lax.broadcasted_iota(jnp.int32, (tile, tile), 0)
    col = lax.broadcasted_iota(jnp.int32, (tile, tile), 1)
    causal = col < row
    sign_bit = jnp.uint32(0x80000000)

    heads = range(SB_GROUP)
    lanes = lambda hh: slice(hh * HEAD_DIM, (hh + 1) * HEAD_DIM)

    def block_rounds(hh, q, kb, first):
        rows = pl.ds(pl.multiple_of(kb * tile, tile), tile)
        z = _dot_nt(q, k_ref[rows, lanes(hh)]) * scale
        yield
        neg_abs = lax.bitcast_convert_type(lax.bitcast_convert_type(z, jnp.uint32) | sign_bit, F32)
        log_beta = jnp.minimum(z, 0.0) - jnp.log(1.0 + jnp.exp(neg_abs))
        log_stay = log_beta - z
        if first:
            log_stay = jnp.where(causal, log_stay, 0.0)
        laters = [None] * n_sub
        run = None if first else carry_s[hh]
        for s in reversed(range(n_sub)):
            ls = log_stay[:, s * sub:(s + 1) * sub]
            cum = _dot(ls.astype(BF16), suffix)
            laters[s] = cum if run is None else cum + run
            tot = jnp.sum(ls, axis=-1, keepdims=True)
            run = tot if run is None else run + tot
        carry_s[hh] = run
        yield
        w = jnp.exp(log_beta + jnp.concatenate(laters, axis=1))
        if first:
            w = jnp.where(causal, w, 0.0)
        pv = _dot(w.astype(BF16), v_ref[rows, lanes(hh)])
        if first:
            acc_s[hh] = pv
        else:
            acc_s[hh] += pv
        yield

    def q_block(qi, c):
        qrows = pl.ds(pl.multiple_of(qi * tile, tile), tile)
        qs = [q_ref[qrows, lanes(hh)] for hh in heads]
        _interleave(*[block_rounds(hh, qs[hh], qi, True) for hh in heads])

        def k_step(s, c2):
            _interleave(*[block_rounds(hh, qs[hh], qi - 1 - s, False) for hh in heads])
            return c2

        lax.fori_loop(0, qi, k_step, 0)
        for hh in heads:
            o_ref[qrows, lanes(hh)] = acc_s[hh].astype(o_ref.dtype)
        return c

    lax.fori_loop(0, seq // tile, q_block, 0)


def sb_attention(qkv, batch, seq, *, tile=512, sub=256, name):
    tile = min(tile, seq)
    sub = min(sub, tile)
    n_groups = SB_HEADS // SB_GROUP
    width = SB_GROUP * HEAD_DIM
    spec = lambda part: pl.BlockSpec((seq, width), lambda b, g: (b, part * n_groups + g))
    return pl.pallas_call(
        functools.partial(_sb_attn_kernel, seq=seq, tile=tile, sub=sub),
        grid=(batch, n_groups),
        in_specs=[spec(0), spec(1), spec(2)],
        out_specs=pl.BlockSpec((seq, width), lambda b, g: (b, g)),
        out_shape=jax.ShapeDtypeStruct((batch * seq, SB_WIDTH), BF16),
        scratch_shapes=[pltpu.VMEM((SB_GROUP, tile, HEAD_DIM), F32), pltpu.VMEM((SB_GROUP, tile, 1), F32)],
        compiler_params=_params(2), name=name,
    )(qkv, qkv, qkv)


def _gdn_kernel(alog_ref, dtb_ref, xq_ref, xk_ref, xv_ref, z_ref, cwq_ref, cwk_ref, cwv_ref,
                brow_ref, arow_ref, bcol_ref, acol_ref, ng_ref, o_ref,
                u_s, wq_s, kt_s, at_s, gcr_s, gcol_s, *, layer, seq):
    sup = min(GDN_SUPER, seq)
    per = sup // CHUNK
    n_sup = seq // sup
    group = pl.program_id(1)
    lanes = lambda hh: slice(hh * HEAD_DIM, (hh + 1) * HEAD_DIM)

    ri = lax.broadcasted_iota(jnp.int32, (sup, sup), 0)
    ci = lax.broadcasted_iota(jnp.int32, (sup, sup), 1)
    same = (ri // CHUNK) == (ci // CHUNK)
    in_lower = jnp.logical_and(same, ri >= ci)
    neg_inf_mask = jnp.where(in_lower, 0.0, -jnp.inf)
    neg_strict = jnp.where(jnp.logical_and(same, ri > ci), -1.0, 0.0)
    tri_lower = jnp.where(in_lower, 1.0, 0.0)
    tri_upper = jnp.where(jnp.logical_and(same, ri <= ci), 1.0, 0.0)
    block_ones = jnp.where(same, 1.0, 0.0)
    sup_lane = lax.broadcasted_iota(jnp.int32, (sup, n_sup), 1)

    def l2norm(y):
        return y * lax.rsqrt(jnp.sum(y * y, axis=-1, keepdims=True) + EPS)

    def conv_silu(x_ref, cw_ref, hh, sc):
        cw = cw_ref[:, lanes(hh)]
        xb = x_ref[pl.ds(_aligned(sc * sup, sup), sup), lanes(hh)]
        if isinstance(sc, int) and sc == 0:
            prev = jnp.zeros((8, HEAD_DIM), F32)
        else:
            prev = x_ref[pl.ds(pl.multiple_of(jnp.maximum(sc * sup - 8, 0), 8), 8), lanes(hh)]
            prev = jnp.where(sc > 0, prev, 0.0)
        xx = jnp.concatenate([prev, xb], axis=0)
        y = cw[SHORT_CONV - 1:SHORT_CONV, :] * xx[8:, :]
        for tap in range(SHORT_CONV - 1):
            shift = SHORT_CONV - 1 - tap
            y = y + cw[tap:tap + 1, :] * xx[8 - shift:8 - shift + sup, :]
        return _silu(y)

    hdot = functools.partial(jnp.dot, precision=lax.Precision.HIGHEST, preferred_element_type=F32)
    for hh in range(GDN_GROUP):
        head = group * GDN_GROUP + hh
        neg_a = -jnp.exp(jnp.full((1, 1), alog_ref[layer, head], F32))
        dtb = dtb_ref[layer, head]
        g_row = neg_a * _softplus(arow_ref[hh] + dtb)
        g_col = neg_a * _softplus(acol_ref[hh] + dtb)
        gcr_s[hh] = hdot(g_row, tri_upper)
        gcol_s[hh, 0] = _sigmoid(bcol_ref[hh])
        gcol_s[hh, 1] = hdot(tri_lower, g_col)
        gcol_s[hh, 2] = hdot(block_ones, g_col)

    heads = range(GDN_GROUP)
    n_sq = max(CHUNK.bit_length() - 2, 0)

    def prep_rounds(sc, slot):
        st = [dict() for _ in heads]
        for hh in heads:
            s = st[hh]
            sel = sup_lane == sc
            pick = lambda kind: jnp.sum(jnp.where(sel, gcol_s[hh, kind], 0.0), axis=1, keepdims=True)
            bcol, gcol, glast = pick(0), pick(1), pick(2)
            grow = gcr_s[hh, pl.ds(sc, 1), :]
            qc = l2norm(conv_silu(xq_ref, cwq_ref, hh, sc)) * (HEAD_DIM ** -0.5)
            kc = l2norm(conv_silu(xk_ref, cwk_ref, hh, sc))
            s["decay"] = jnp.exp((gcol - grow) + neg_inf_mask)
            kb = kc * bcol
            egc = jnp.exp(gcol)
            s["rhs"] = jnp.concatenate([conv_silu(xv_ref, cwv_ref, hh, sc) * bcol, kb * egc], axis=1)
            s["kb"], s["kc"], s["qc"] = kb.astype(BF16), kc.astype(BF16), qc.astype(BF16)
            qd = (qc * egc).astype(BF16)
            for p in range(per):
                blk = slice(p * CHUNK, (p + 1) * CHUNK)
                wq_s[slot, hh, p, CHUNK:, :] = qd[blk, :]
            kt_s[slot, hh] = (kc * jnp.exp(glast - gcol)).astype(BF16)
        yield
        for hh in heads:
            s = st[hh]
            s["y"] = _dot_nt(s["kb"], s["kc"]) * s["decay"] * neg_strict
            s["z"] = s["y"]
            attn = (_dot_nt(s["qc"], s["kc"]) * s["decay"]).astype(BF16)
            for p in range(per):
                blk = slice(p * CHUNK, (p + 1) * CHUNK)
                at_s[slot, hh, blk, :] = attn[blk, blk]
        yield
        for m in range(n_sq):
            for hh in heads:
                s = st[hh]
                yb = s["y"].astype(BF16)
                if m > 0:
                    s["z"] = s["z"] + s["y"] + _dot(s["z"].astype(BF16), yb)
                s["y"] = _dot(yb, yb)
            yield
        for hh in heads:
            s = st[hh]
            s["z"] = s["z"] + s["y"] + _dot(s["z"].astype(BF16), s["y"].astype(BF16))
        yield
        for hh in heads:
            s = st[hh]
            uw = s["rhs"] + _dot(s["z"].astype(BF16), s["rhs"].astype(BF16))
            u_s[slot, hh] = uw[:, :HEAD_DIM]
            w = uw[:, HEAD_DIM:].astype(BF16)
            for p in range(per):
                blk = slice(p * CHUNK, (p + 1) * CHUNK)
                wq_s[slot, hh, p, :CHUNK, :] = w[blk, :]
        yield

    ng = ng_ref[...]

    def scan_rounds(sc, slot, states):
        chunk_decay = [jnp.exp(gcr_s[hh, pl.ds(sc, 1), :]) for hh in heads]
        for p in range(per):
            blk = slice(p * CHUNK, (p + 1) * CHUNK)
            rows = pl.ds(_aligned(sc * sup + p * CHUNK, CHUNK), CHUNK)
            vnb, qs = [None] * GDN_GROUP, [None] * GDN_GROUP
            for hh in heads:
                both = _dot(wq_s[slot, hh, p], states[hh].astype(BF16))
                vnb[hh] = (u_s[slot, hh, blk, :] - both[:CHUNK, :]).astype(BF16)
                qs[hh] = both[CHUNK:, :]
            yield
            for hh in heads:
                o = qs[hh] + _dot(at_s[slot, hh, blk, :], vnb[hh])
                last = p * CHUNK + CHUNK - 1
                cd = chunk_decay[hh][:, last:last + 1]
                states[hh] = states[hh] * cd + _dot_tn(kt_s[slot, hh, blk, :], vnb[hh])
                o_ref[rows, lanes(hh)] = (_rmsnorm(o, ng) * _silu(z_ref[rows, lanes(hh)])).astype(o_ref.dtype)
            yield

    def step(i, states):
        states = list(states)
        slot = lax.rem(i, 2)
        _interleave(prep_rounds(i, slot), scan_rounds(i - 1, 1 - slot, states))
        return tuple(states)

    _interleave(prep_rounds(0, 0))
    states = tuple(jnp.zeros((HEAD_DIM, HEAD_DIM), F32) for _ in heads)
    states = list(lax.fori_loop(1, n_sup, step, states))
    _interleave(scan_rounds(n_sup - 1, (n_sup - 1) % 2, states))


def gated_deltanet(qkvz, ba_rows, ba_cols, conv_w, a_log, dt_bias, norm_g, layer, batch, seq, *, name):
    sup = min(GDN_SUPER, seq)
    n_sup = seq // sup
    n_groups = GDN_HEADS // GDN_GROUP
    width = GDN_GROUP * HEAD_DIM
    col = lambda part: pl.BlockSpec((seq, width), lambda b, g: (b, part * n_groups + g))
    cw = lambda part: pl.BlockSpec((None, SHORT_CONV, width), lambda b, g: (layer, 0, part * n_groups + g))
    rows = lambda part: pl.BlockSpec((None, GDN_GROUP, n_sup, sup), lambda b, g: (b, part * n_groups + g, 0, 0))
    cols = lambda part: pl.BlockSpec((None, GDN_GROUP, sup, n_sup), lambda b, g: (b, part * n_groups + g, 0, 0))
    smem = pl.BlockSpec(memory_space=pltpu.SMEM)
    per_head = lambda shape, dt: pltpu.VMEM((GDN_GROUP,) + shape, dt)
    slots = lambda shape, dt: pltpu.VMEM((2, GDN_GROUP) + shape, dt)
    return pl.pallas_call(
        functools.partial(_gdn_kernel, layer=layer, seq=seq),
        grid=(batch, n_groups),
        in_specs=[smem, smem, col(0), col(1), col(2), col(3), cw(0), cw(1), cw(2),
                  rows(0), rows(1), cols(0), cols(1),
                  pl.BlockSpec((None, 1, HEAD_DIM), lambda b, g: (layer, 0, 0))],
        out_specs=pl.BlockSpec((seq, width), lambda b, g: (b, g)),
        out_shape=jax.ShapeDtypeStruct((batch * seq, GDN_WIDTH), BF16),
        scratch_shapes=[
            slots((sup, HEAD_DIM), F32),
            slots((sup // CHUNK, 2 * CHUNK, HEAD_DIM), BF16),
            slots((sup, HEAD_DIM), BF16),
            slots((sup, CHUNK), BF16),
            per_head((n_sup, sup), F32),
            per_head((3, sup, n_sup), F32),
        ],
        compiler_params=_params(2), name=name,
    )(a_log, dt_bias, qkvz, qkvz, qkvz, qkvz, conv_w, conv_w, conv_w,
      ba_rows, ba_rows, ba_cols, ba_cols, norm_g)


def _proj2_residual_kernel(x_ref, a1_ref, a2_ref, w1_ref, w2_ref, o_ref):
    o_ref[...] = x_ref[...] + _dot(a1_ref[...], w1_ref[...]) + _dot(a2_ref[...], w2_ref[...])


def proj2_residual(x, a1, a2, w, layer, *, tm, tn, name):
    m, n = x.shape
    k1, k2 = a1.shape[1], a2.shape[1]
    assert k1 == k2
    tm = min(tm, m)
    return pl.pallas_call(
        _proj2_residual_kernel,
        grid=(m // tm, n // tn),
        in_specs=[
            pl.BlockSpec((tm, tn), lambda i, j: (i, j)),
            pl.BlockSpec((tm, k1), lambda i, j: (i, 0)),
            pl.BlockSpec((tm, k2), lambda i, j: (i, 0)),
            pl.BlockSpec((None, k1, tn), lambda i, j: (layer, 0, j)),
            pl.BlockSpec((None, k2, tn), lambda i, j: (layer, 1, j)),
        ],
        out_specs=pl.BlockSpec((tm, tn), lambda i, j: (i, j)),
        out_shape=jax.ShapeDtypeStruct((m, n), F32),
        compiler_params=_params(2), name=name,
    )(x, a1, a2, w, w)


def _xattn_kernel(x_ref, g_ref, wq_ref, k_ref, v_ref, wo_ref, o_ref, h_s, a_s, *, tm):
    _rmsnorm_into(h_s, 0, x_ref, g_ref[...], tm)
    q = _dot(h_s[...], wq_ref[...]).astype(BF16)
    scale = HEAD_DIM ** -0.5
    for hd in range(X_HEADS):
        cols = slice(hd * HEAD_DIM, (hd + 1) * HEAD_DIM)
        s = _dot_nt(q[:, cols], k_ref[:, cols]) * scale
        p = jnp.exp(s - jnp.max(s, axis=-1, keepdims=True))
        denom = jnp.sum(p, axis=-1, keepdims=True)
        a_s[:, cols] = (_dot(p.astype(BF16), v_ref[:, cols]) / denom).astype(BF16)
    o_ref[...] = x_ref[...] + _dot(a_s[...], wo_ref[...])


def cross_attention_block(x, g, wq, kv, wo, layer, seq, n_mem, *, tm, name):
    m, d = x.shape
    tm = min(tm, seq)
    per_seq = seq // tm
    return pl.pallas_call(
        functools.partial(_xattn_kernel, tm=tm),
        grid=(m // tm,),
        in_specs=[
            pl.BlockSpec((tm, d), lambda i: (i, 0)),
            pl.BlockSpec((None, 1, d), lambda i: (layer, 0, 0)),
            pl.BlockSpec((None, d, X_WIDTH), lambda i: (layer, 0, 0)),
            pl.BlockSpec((n_mem, X_WIDTH), lambda i: (i // per_seq, 0)),
            pl.BlockSpec((n_mem, X_WIDTH), lambda i: (i // per_seq, 1)),
            pl.BlockSpec((None, X_WIDTH, d), lambda i: (layer, 0, 0)),
        ],
        out_specs=pl.BlockSpec((tm, d), lambda i: (i, 0)),
        out_shape=jax.ShapeDtypeStruct((m, d), F32),
        scratch_shapes=[pltpu.VMEM((tm, d), BF16), pltpu.VMEM((tm, X_WIDTH), BF16)],
        compiler_params=_params(1), name=name,
    )(x, g, wq, kv, kv, wo)


def _ffn_up_kernel(x_ref, halo_ref, g_ref, wg_ref, wu_ref, cg_ref, cu_ref, bg_ref, bu_ref,
                   o_ref, h_s, *, tm, tiles_per_seq):
    i = pl.program_id(0)

    @pl.when(pl.program_id(1) == 0)
    def _():
        keep = jnp.where(i % tiles_per_seq == 0, 0.0, 1.0)
        _rmsnorm_into(h_s, 0, halo_ref, g_ref[...], HALO_ROWS, scale=keep)
        _rmsnorm_into(h_s, HALO_ROWS, x_ref, g_ref[...], tm)

    def conv(y, cw_ref, b_ref):
        cw = cw_ref[...]
        out = cw[FFN_CONV - 1:FFN_CONV, :] * y[HALO_ROWS:, :]
        for tap in range(FFN_CONV - 1):
            shift = FFN_CONV - 1 - tap
            out = out + cw[tap:tap + 1, :] * y[HALO_ROWS - shift:HALO_ROWS - shift + tm, :]
        return out + b_ref[...]

    h = h_s[...]
    gate = conv(_dot(h, wg_ref[...]), cg_ref, bg_ref)
    up = conv(_dot(h, wu_ref[...]), cu_ref, bu_ref)
    o_ref[...] = (_silu(gate) * up).astype(o_ref.dtype)


def ffn_up(x, g, w_up, conv_w, conv_b, layer, seq, *, tm, tn, name):
    m, d = x.shape
    d_ff = w_up.shape[-1] // 2
    tm = min(tm, seq)
    assert d_ff % tn == 0 and seq % tm == 0 and tm % HALO_ROWS == 0
    nj = d_ff // tn
    halo_blocks = tm // HALO_ROWS
    return pl.pallas_call(
        functools.partial(_ffn_up_kernel, tm=tm, tiles_per_seq=seq // tm),
        grid=(m // tm, nj),
        in_specs=[
            pl.BlockSpec((tm, d), lambda i, j: (i, 0)),
            pl.BlockSpec((HALO_ROWS, d), lambda i, j: (jnp.maximum(i * halo_blocks - 1, 0), 0)),
            pl.BlockSpec((None, 1, d), lambda i, j: (layer, 0, 0)),
            pl.BlockSpec((None, d, tn), lambda i, j: (layer, 0, j)),
            pl.BlockSpec((None, d, tn), lambda i, j: (layer, 0, nj + j)),
            pl.BlockSpec((None, FFN_CONV, tn), lambda i, j: (layer, 0, j)),
            pl.BlockSpec((None, FFN_CONV, tn), lambda i, j: (layer, 0, nj + j)),
            pl.BlockSpec((None, 1, tn), lambda i, j: (layer, 0, j)),
            pl.BlockSpec((None, 1, tn), lambda i, j: (layer, 0, nj + j)),
        ],
        out_specs=pl.BlockSpec((tm, tn), lambda i, j: (i, j)),
        out_shape=jax.ShapeDtypeStruct((m, d_ff), BF16),
        scratch_shapes=[pltpu.VMEM((tm + HALO_ROWS, d), BF16)],
        compiler_params=_params(2), name=name,
    )(x, x, g, w_up, w_up, conv_w, conv_w, conv_b, conv_b)


def _matmul_residual_kernel(x_ref, a_ref, w_ref, o_ref):
    o_ref[...] = x_ref[...] + _dot(a_ref[...], w_ref[...])


def matmul_residual(x, a, w, layer, *, tm, tn, name):
    m, n = x.shape
    k = a.shape[1]
    tm = min(tm, m)
    return pl.pallas_call(
        _matmul_residual_kernel,
        grid=(m // tm, n // tn),
        in_specs=[
            pl.BlockSpec((tm, tn), lambda i, j: (i, j)),
            pl.BlockSpec((tm, k), lambda i, j: (i, 0)),
            pl.BlockSpec((None, k, tn), lambda i, j: (layer, 0, j)),
        ],
        out_specs=pl.BlockSpec((tm, tn), lambda i, j: (i, j)),
        out_shape=jax.ShapeDtypeStruct((m, n), F32),
        compiler_params=_params(2), name=name,
    )(x, a, w)


def _final_norm_kernel(x_ref, g_ref, o_ref, *, tm):
    _rmsnorm_into(o_ref, 0, x_ref, g_ref[...], tm)


def final_norm(x, g, *, tm, name):
    m, d = x.shape
    tm = min(tm, m)
    return pl.pallas_call(
        functools.partial(_final_norm_kernel, tm=tm),
        grid=(m // tm,),
        in_specs=[pl.BlockSpec((tm, d), lambda i: (i, 0)), pl.BlockSpec((1, d), lambda i: (0, 0))],
        out_specs=pl.BlockSpec((tm, d), lambda i: (i, 0)),
        out_shape=jax.ShapeDtypeStruct((m, d), F32),
        compiler_params=_params(1), name=name,
    )(x, g)


def kernel(x, mem, mix_norm, w_in, gdn_conv, gdn_a_log, gdn_dt_bias, gdn_norm, w_out, xattn_norm,
           mem_norm, w_xq, w_xkv, w_xo, ffn_norm, w_up, ffn_conv, ffn_conv_bias, w_down, final_norm_g):
    batch, seq, d_model = x.shape
    n_mem = mem.shape[1]
    depth = w_in.shape[0]
    sup = min(GDN_SUPER, seq)
    n_main = 3 * SB_WIDTH + 4 * GDN_WIDTH
    n_gate = 2 * GDN_HEADS

    w_in_b = w_in.astype(BF16)
    w_gate_b = jnp.pad(w_in[:, :, n_main:], ((0, 0), (0, 0), (0, HEAD_DIM - n_gate))).astype(BF16)
    w_out_b = w_out.astype(BF16)
    w_xq_b = w_xq.astype(BF16)
    w_xkv_b = w_xkv.astype(BF16)
    w_xo_b = w_xo.astype(BF16)
    w_up_b = w_up.astype(BF16)
    w_down_b = w_down.astype(BF16)
    row3 = lambda a: a.reshape(depth, 1, a.shape[-1])
    mix_g, xat_g, mem_g, ffn_g = row3(mix_norm), row3(xattn_norm), row3(mem_norm), row3(ffn_norm)
    gdn_g = row3(gdn_norm)
    conv_b = row3(ffn_conv_bias)

    xf = x.reshape(batch * seq, d_model)
    memf = mem.reshape(batch * n_mem, d_model)

    for l in range(depth):
        sb_qkv = norm_matmul(xf, mix_g, w_in_b, l, 0, 3 * SB_WIDTH, BF16, tm=1024, tn=1024,
                             name=f"inproj_sb_{l}")
        gdn_qkvz, gates = norm_matmul(xf, mix_g, w_in_b, l, 3 * SB_WIDTH, 4 * GDN_WIDTH, F32,
                                      tm=1024, tn=1024, side_w=w_gate_b, name=f"inproj_gdn_{l}")
        sb_out = sb_attention(sb_qkv, batch, seq, name=f"sb_attn_{l}")
        ba = gates[:, :n_gate].reshape(batch, seq, n_gate).transpose(0, 2, 1)
        ba_rows = ba.reshape(batch, n_gate, seq // sup, sup)
        ba_cols = ba_rows.transpose(0, 1, 3, 2)
        gdn_out = gated_deltanet(gdn_qkvz, ba_rows, ba_cols, gdn_conv, gdn_a_log, gdn_dt_bias, gdn_g,
                                 l, batch, seq, name=f"gdn_{l}")
        xf = proj2_residual(xf, sb_out, gdn_out, w_out_b, l, tm=1024, tn=1024, name=f"outproj_{l}")

        kv = norm_matmul(memf, mem_g, w_xkv_b, l, 0, 2 * X_WIDTH, BF16, tm=512, tn=512,
                         name=f"mem_kv_{l}")
        xf = cross_attention_block(xf, xat_g, w_xq_b, kv, w_xo_b, l, seq, n_mem, tm=512,
                                   name=f"xattn_{l}")

        act = ffn_up(xf, ffn_g, w_up_b, ffn_conv, conv_b, l, seq, tm=1024, tn=512, name=f"ffn_up_{l}")
        xf = matmul_residual(xf, act, w_down_b, l, tm=1024, tn=512, name=f"ffn_down_{l}")

    out = final_norm(xf, final_norm_g.reshape(1, d_model), tm=512, name="final_norm")
    return out.reshape(batch, seq, d_model)
```

```python
import functools
import math

import jax
import jax.numpy as jnp
from jax import lax
from jax.experimental import pallas as pl
from jax.experimental.pallas import tpu as pltpu

F32 = jnp.float32
BF16 = jnp.bfloat16

HEAD_DIM = 128
SB_HEADS = 8
SB_GROUP = 2
GDN_HEADS = 8
SB_WIDTH = SB_HEADS * HEAD_DIM
GDN_WIDTH = GDN_HEADS * HEAD_DIM
SHORT_CONV = 4
CHUNK = 64
GDN_SUPER = 256
GDN_GROUP = 4
X_HEADS = 4
X_WIDTH = X_HEADS * HEAD_DIM
FFN_CONV = 3
EPS = 1e-6

V7X_VMEM_BYTES = 64 * 1024 * 1024
VMEM_LIMIT_BYTES = V7X_VMEM_BYTES - 8 * 1024 * 1024
NORM_ROWS = 64
HALO_ROWS = 16
COPY_ROWS = 128


def _params(n_axes):
    return pltpu.CompilerParams(
        dimension_semantics=("arbitrary",) * n_axes, vmem_limit_bytes=VMEM_LIMIT_BYTES)


def _rmsnorm(x, g):
    ms = jnp.mean(x * x, axis=-1, keepdims=True)
    return x * lax.rsqrt(ms + EPS) * g


def _rmsnorm_into(dst_ref, dst_row0, x_ref, g, n_rows, scale=None):
    step = min(NORM_ROWS, n_rows)
    dst_align = math.gcd(step, dst_row0) if dst_row0 else step

    def body(r, carry):
        src = pl.ds(pl.multiple_of(r * step, step), step)
        dst = pl.ds(pl.multiple_of(dst_row0 + r * step, dst_align), step)
        y = _rmsnorm(x_ref[src, :], g)
        if scale is not None:
            y = y * scale
        dst_ref[dst, :] = y.astype(dst_ref.dtype)
        return carry

    lax.fori_loop(0, n_rows // step, body, 0)


def _aligned(start, multiple):
    return start if isinstance(start, int) else pl.multiple_of(start, multiple)


def _interleave(*gens):
    live = list(gens)
    while live:
        for g in list(live):
            if next(g, live) is live:
                live.remove(g)


def _sigmoid(x):
    return 1.0 / (1.0 + jnp.exp(-x))


def _silu(x):
    return x * _sigmoid(x)


def _softplus(x):
    return jnp.maximum(x, 0.0) + jnp.log1p(jnp.exp(-jnp.abs(x)))


def _dot(a, b):
    return lax.dot_general(a, b, (((1,), (0,)), ((), ())), preferred_element_type=F32)


def _dot_nt(a, b):
    return lax.dot_general(a, b, (((1,), (1,)), ((), ())), preferred_element_type=F32)


def _dot_tn(a, b):
    return lax.dot_general(a, b, (((0,), (0,)), ((), ())), preferred_element_type=F32)


def _norm_matmul_kernel(x_ref, g_ref, w_ref, o_ref, h_ref, *, tm):
    @pl.when(pl.program_id(1) == 0)
    def _():
        _rmsnorm_into(h_ref, 0, x_ref, g_ref[...], tm)

    o_ref[...] = _dot(h_ref[...], w_ref[...]).astype(o_ref.dtype)


def norm_matmul(x, g, w, layer, out_dtype, *, tm, tn, name):
    m, k = x.shape
    n = w.shape[-1]
    tm = min(tm, m)
    assert m % tm == 0 and n % tn == 0
    return pl.pallas_call(
        functools.partial(_norm_matmul_kernel, tm=tm),
        grid=(m // tm, n // tn),
        in_specs=[
            pl.BlockSpec((tm, k), lambda i, j: (i, 0)),
            pl.BlockSpec((None, 1, k), lambda i, j: (layer, 0, 0)),
            pl.BlockSpec((None, k, tn), lambda i, j: (layer, 0, j)),
        ],
        out_specs=pl.BlockSpec((tm, tn), lambda i, j: (i, j)),
        out_shape=jax.ShapeDtypeStruct((m, n), out_dtype),
        scratch_shapes=[pltpu.VMEM((tm, k), BF16)],
        compiler_params=_params(2), name=name,
    )(x, g, w)


def _in_proj_kernel(x_ref, g_ref, w_ref, ws_ref, osb_ref, ogdn_ref, ogate_ref, h_ref, *, tm, n_sb):
    j = pl.program_id(1)

    @pl.when(j == 0)
    def _():
        _rmsnorm_into(h_ref, 0, x_ref, g_ref[...], tm)
        ogate_ref[...] = _dot(h_ref[...], ws_ref[...])

    @pl.when(j < n_sb)
    def _():
        osb_ref[...] = _dot(h_ref[...], w_ref[...]).astype(osb_ref.dtype)

    @pl.when(j >= n_sb)
    def _():
        ogdn_ref[...] = _dot(h_ref[...], w_ref[...])


def in_projection(x, g, w, w_gate, layer, n_sb_cols, n_gdn_cols, *, tm, tn, name):
    m, k = x.shape
    tm = min(tm, m)
    assert m % tm == 0 and n_sb_cols % tn == 0 and n_gdn_cols % tn == 0
    n_sb, n_gdn = n_sb_cols // tn, n_gdn_cols // tn
    ns = w_gate.shape[-1]
    return pl.pallas_call(
        functools.partial(_in_proj_kernel, tm=tm, n_sb=n_sb),
        grid=(m // tm, n_sb + n_gdn),
        in_specs=[
            pl.BlockSpec((tm, k), lambda i, j: (i, 0)),
            pl.BlockSpec((None, 1, k), lambda i, j: (layer, 0, 0)),
            pl.BlockSpec((None, k, tn), lambda i, j: (layer, 0, j)),
            pl.BlockSpec((None, k, ns), lambda i, j: (layer, 0, 0)),
        ],
        out_specs=[
            pl.BlockSpec((tm, tn), lambda i, j: (i, jnp.minimum(j, n_sb - 1))),
            pl.BlockSpec((tm, tn), lambda i, j: (i, jnp.maximum(j - n_sb, 0))),
            pl.BlockSpec((tm, ns), lambda i, j: (i, 0)),
        ],
        out_shape=[jax.ShapeDtypeStruct((m, n_sb_cols), BF16),
                   jax.ShapeDtypeStruct((m, n_gdn_cols), F32),
                   jax.ShapeDtypeStruct((m, ns), F32)],
        scratch_shapes=[pltpu.VMEM((tm, k), BF16)],
        compiler_params=_params(2), name=name,
    )(x, g, w, w_gate)


def _sb_attn_kernel(q_ref, k_ref, v_ref, o_ref, acc_s, carry_s, *, seq, tile, sub):
    scale = HEAD_DIM ** -0.5
    n_sub = tile // sub
    r = lax.broadcasted_iota(jnp.int32, (sub, sub), 0)
    c = lax.broadcasted_iota(jnp.int32, (sub, sub), 1)
    suffix = jnp.where(r > c, 1.0, 0.0).astype(BF16)
    row = lax.broadcasted_iota(jnp.int32, (tile, tile), 0)
    col = lax.broadcasted_iota(jnp.int32, (tile, tile), 1)
    causal = col < row
    sign_bit = jnp.uint32(0x80000000)

    heads = range(SB_GROUP)
    lanes = lambda hh: slice(hh * HEAD_DIM, (hh + 1) * HEAD_DIM)

    def block_rounds(hh, q, kb, first):
        rows = pl.ds(pl.multiple_of(kb * tile, tile), tile)
        z = _dot_nt(q, k_ref[rows, lanes(hh)]) * scale
        yield
        neg_abs = lax.bitcast_convert_type(lax.bitcast_convert_type(z, jnp.uint32) | sign_bit, F32)
        log_beta = jnp.minimum(z, 0.0) - jnp.log(1.0 + jnp.exp(neg_abs))
        log_stay = log_beta - z
        if first:
            log_stay = jnp.where(causal, log_stay, 0.0)
        laters = [None] * n_sub
        run = None if first else carry_s[hh]
        for s in reversed(range(n_sub)):
            ls = log_stay[:, s * sub:(s + 1) * sub]
            cum = _dot(ls.astype(BF16), suffix)
            laters[s] = cum if run is None else cum + run
            tot = jnp.sum(ls, axis=-1, keepdims=True)
            run = tot if run is None else run + tot
        carry_s[hh] = run
        yield
        w = jnp.exp(log_beta + jnp.concatenate(laters, axis=1))
        if first:
            w = jnp.where(causal, w, 0.0)
        pv = _dot(w.astype(BF16), v_ref[rows, lanes(hh)])
        if first:
            acc_s[hh] = pv
        else:
            acc_s[hh] += pv
        yield

    def q_block(qi, c):
        qrows = pl.ds(pl.multiple_of(qi * tile, tile), tile)
        qs = [q_ref[qrows, lanes(hh)] for hh in heads]
        _interleave(*[block_rounds(hh, qs[hh], qi, True) for hh in heads])

        def k_step(s, c2):
            _interleave(*[block_rounds(hh, qs[hh], qi - 1 - s, False) for hh in heads])
            return c2

        lax.fori_loop(0, qi, k_step, 0)
        for hh in heads:
            o_ref[qrows, lanes(hh)] = acc_s[hh].astype(o_ref.dtype)
        return c

    lax.fori_loop(0, seq // tile, q_block, 0)


def sb_attention(qkv, batch, seq, *, tile=512, sub=256, name):
    tile = min(tile, seq)
    sub = min(sub, tile)
    n_groups = SB_HEADS // SB_GROUP
    width = SB_GROUP * HEAD_DIM
    spec = lambda part: pl.BlockSpec((seq, width), lambda b, g: (b, part * n_groups + g))
    return pl.pallas_call(
        functools.partial(_sb_attn_kernel, seq=seq, tile=tile, sub=sub),
        grid=(batch, n_groups),
        in_specs=[spec(0), spec(1), spec(2)],
        out_specs=pl.BlockSpec((seq, width), lambda b, g: (b, g)),
        out_shape=jax.ShapeDtypeStruct((batch * seq, SB_WIDTH), BF16),
        scratch_shapes=[pltpu.VMEM((SB_GROUP, tile, HEAD_DIM), F32), pltpu.VMEM((SB_GROUP, tile, 1), F32)],
        compiler_params=_params(2), name=name,
    )(qkv, qkv, qkv)


def _gdn_kernel(alog_ref, dtb_ref, xq_ref, xk_ref, xv_ref, z_ref, cwq_ref, cwk_ref, cwv_ref,
                brow_ref, arow_ref, bcol_ref, acol_ref, ng_ref, o_ref,
                u_s, wq_s, kt_s, at_s, gcr_s, gcol_s, *, layer, seq):
    sup = min(GDN_SUPER, seq)
    per = sup // CHUNK
    n_sup = seq // sup
    group = pl.program_id(1)
    lanes = lambda hh: slice(hh * HEAD_DIM, (hh + 1) * HEAD_DIM)

    ri = lax.broadcasted_iota(jnp.int32, (sup, sup), 0)
    ci = lax.broadcasted_iota(jnp.int32, (sup, sup), 1)
    same = (ri // CHUNK) == (ci // CHUNK)
    in_lower = jnp.logical_and(same, ri >= ci)
    neg_inf_mask = jnp.where(in_lower, 0.0, -jnp.inf)
    neg_strict = jnp.where(jnp.logical_and(same, ri > ci), -1.0, 0.0)
    tri_lower = jnp.where(in_lower, 1.0, 0.0)
    tri_upper = jnp.where(jnp.logical_and(same, ri <= ci), 1.0, 0.0)
    block_ones = jnp.where(same, 1.0, 0.0)
    sup_lane = lax.broadcasted_iota(jnp.int32, (sup, n_sup), 1)

    def l2norm(y):
        return y * lax.rsqrt(jnp.sum(y * y, axis=-1, keepdims=True) + EPS)

    def conv_silu(x_ref, cw_ref, hh, sc):
        cw = cw_ref[:, lanes(hh)]
        xb = x_ref[pl.ds(_aligned(sc * sup, sup), sup), lanes(hh)]
        if isinstance(sc, int) and sc == 0:
            prev = jnp.zeros((8, HEAD_DIM), F32)
        else:
            prev = x_ref[pl.ds(pl.multiple_of(jnp.maximum(sc * sup - 8, 0), 8), 8), lanes(hh)]
            prev = jnp.where(sc > 0, prev, 0.0)
        xx = jnp.concatenate([prev, xb], axis=0)
        y = cw[SHORT_CONV - 1:SHORT_CONV, :] * xx[8:, :]
        for tap in range(SHORT_CONV - 1):
            shift = SHORT_CONV - 1 - tap
            y = y + cw[tap:tap + 1, :] * xx[8 - shift:8 - shift + sup, :]
        return _silu(y)

    hdot = functools.partial(jnp.dot, precision=lax.Precision.HIGHEST, preferred_element_type=F32)
    for hh in range(GDN_GROUP):
        head = group * GDN_GROUP + hh
        neg_a = -jnp.exp(jnp.full((1, 1), alog_ref[layer, head], F32))
        dtb = dtb_ref[layer, head]
        g_row = neg_a * _softplus(arow_ref[hh] + dtb)
        g_col = neg_a * _softplus(acol_ref[hh] + dtb)
        gcr_s[hh] = hdot(g_row, tri_upper)
        gcol_s[hh, 0] = _sigmoid(bcol_ref[hh])
        gcol_s[hh, 1] = hdot(tri_lower, g_col)
        gcol_s[hh, 2] = hdot(block_ones, g_col)

    heads = range(GDN_GROUP)
    n_sq = max(CHUNK.bit_length() - 2, 0)

    def prep_rounds(sc, slot):
        st = [dict() for _ in heads]
        for hh in heads:
            s = st[hh]
            sel = sup_lane == sc
            pick = lambda kind: jnp.sum(jnp.where(sel, gcol_s[hh, kind], 0.0), axis=1, keepdims=True)
            bcol, gcol, glast = pick(0), pick(1), pick(2)
            grow = gcr_s[hh, pl.ds(sc, 1), :]
            qc = l2norm(conv_silu(xq_ref, cwq_ref, hh, sc)) * (HEAD_DIM ** -0.5)
            kc = l2norm(conv_silu(xk_ref, cwk_ref, hh, sc))
            s["decay"] = jnp.exp((gcol - grow) + neg_inf_mask)
            kb = kc * bcol
            egc = jnp.exp(gcol)
            s["rhs"] = jnp.concatenate([conv_silu(xv_ref, cwv_ref, hh, sc) * bcol, kb * egc], axis=1)
            s["kb"], s["kc"], s["qc"] = kb.astype(BF16), kc.astype(BF16), qc.astype(BF16)
            qd = (qc * egc).astype(BF16)
            for p in range(per):
                blk = slice(p * CHUNK, (p + 1) * CHUNK)
                wq_s[slot, hh, p, CHUNK:, :] = qd[blk, :]
            kt_s[slot, hh] = (kc * jnp.exp(glast - gcol)).astype(BF16)
        yield
        for hh in heads:
            s = st[hh]
            s["y"] = _dot_nt(s["kb"], s["kc"]) * s["decay"] * neg_strict
            s["z"] = s["y"]
            attn = (_dot_nt(s["qc"], s["kc"]) * s["decay"]).astype(BF16)
            for p in range(per):
                blk = slice(p * CHUNK, (p + 1) * CHUNK)
                at_s[slot, hh, blk, :] = attn[blk, blk]
        yield
        for m in range(n_sq):
            for hh in heads:
                s = st[hh]
                yb = s["y"].astype(BF16)
                if m > 0:
                    s["z"] = s["z"] + s["y"] + _dot(s["z"].astype(BF16), yb)
                s["y"] = _dot(yb, yb)
            yield
        for hh in heads:
            s = st[hh]
            s["z"] = s["z"] + s["y"] + _dot(s["z"].astype(BF16), s["y"].astype(BF16))
        yield
        for hh in heads:
            s = st[hh]
            uw = s["rhs"] + _dot(s["z"].astype(BF16), s["rhs"].astype(BF16))
            u_s[slot, hh] = uw[:, :HEAD_DIM]
            w = uw[:, HEAD_DIM:].astype(BF16)
            for p in range(per):
                blk = slice(p * CHUNK, (p + 1) * CHUNK)
                wq_s[slot, hh, p, :CHUNK, :] = w[blk, :]
        yield

    ng = ng_ref[...]

    def scan_rounds(sc, slot, states):
        chunk_decay = [jnp.exp(gcr_s[hh, pl.ds(sc, 1), :]) for hh in heads]
        for p in range(per):
            blk = slice(p * CHUNK, (p + 1) * CHUNK)
            rows = pl.ds(_aligned(sc * sup + p * CHUNK, CHUNK), CHUNK)
            vnb, qs = [None] * GDN_GROUP, [None] * GDN_GROUP
            for hh in heads:
                both = _dot(wq_s[slot, hh, p], states[hh].astype(BF16))
                vnb[hh] = (u_s[slot, hh, blk, :] - both[:CHUNK, :]).astype(BF16)
                qs[hh] = both[CHUNK:, :]
            yield
            for hh in heads:
                o = qs[hh] + _dot(at_s[slot, hh, blk, :], vnb[hh])
                last = p * CHUNK + CHUNK - 1
                cd = chunk_decay[hh][:, last:last + 1]
                states[hh] = states[hh] * cd + _dot_tn(kt_s[slot, hh, blk, :], vnb[hh])
                o_ref[rows, lanes(hh)] = (_rmsnorm(o, ng) * _silu(z_ref[rows, lanes(hh)])).astype(o_ref.dtype)
            yield

    def step(i, states):
        states = list(states)
        slot = lax.rem(i, 2)
        _interleave(prep_rounds(i, slot), scan_rounds(i - 1, 1 - slot, states))
        return tuple(states)

    _interleave(prep_rounds(0, 0))
    states = tuple(jnp.zeros((HEAD_DIM, HEAD_DIM), F32) for _ in heads)
    states = list(lax.fori_loop(1, n_sup, step, states))
    _interleave(scan_rounds(n_sup - 1, (n_sup - 1) % 2, states))


def gated_deltanet(qkvz, ba_rows, ba_cols, conv_w, a_log, dt_bias, norm_g, layer, batch, seq, *, name):
    sup = min(GDN_SUPER, seq)
    n_sup = seq // sup
    n_groups = GDN_HEADS // GDN_GROUP
    width = GDN_GROUP * HEAD_DIM
    col = lambda part: pl.BlockSpec((seq, width), lambda b, g: (b, part * n_groups + g))
    cw = lambda part: pl.BlockSpec((None, SHORT_CONV, width), lambda b, g: (layer, 0, part * n_groups + g))
    rows = lambda part: pl.BlockSpec((None, GDN_GROUP, n_sup, sup), lambda b, g: (b, part * n_groups + g, 0, 0))
    cols = lambda part: pl.BlockSpec((None, GDN_GROUP, sup, n_sup), lambda b, g: (b, part * n_groups + g, 0, 0))
    smem = pl.BlockSpec(memory_space=pltpu.SMEM)
    per_head = lambda shape, dt: pltpu.VMEM((GDN_GROUP,) + shape, dt)
    slots = lambda shape, dt: pltpu.VMEM((2, GDN_GROUP) + shape, dt)
    return pl.pallas_call(
        functools.partial(_gdn_kernel, layer=layer, seq=seq),
        grid=(batch, n_groups),
        in_specs=[smem, smem, col(0), col(1), col(2), col(3), cw(0), cw(1), cw(2),
                  rows(0), rows(1), cols(0), cols(1),
                  pl.BlockSpec((None, 1, HEAD_DIM), lambda b, g: (layer, 0, 0))],
        out_specs=pl.BlockSpec((seq, width), lambda b, g: (b, g)),
        out_shape=jax.ShapeDtypeStruct((batch * seq, GDN_WIDTH), BF16),
        scratch_shapes=[
            slots((sup, HEAD_DIM), F32),
            slots((sup // CHUNK, 2 * CHUNK, HEAD_DIM), BF16),
            slots((sup, HEAD_DIM), BF16),
            slots((sup, CHUNK), BF16),
            per_head((n_sup, sup), F32),
            per_head((3, sup, n_sup), F32),
        ],
        compiler_params=_params(2), name=name,
    )(a_log, dt_bias, qkvz, qkvz, qkvz, qkvz, conv_w, conv_w, conv_w,
      ba_rows, ba_rows, ba_cols, ba_cols, norm_g)


def _proj2_residual_kernel(x_ref, a1_ref, a2_ref, w1_ref, w2_ref, o_ref):
    o_ref[...] = x_ref[...] + _dot(a1_ref[...], w1_ref[...]) + _dot(a2_ref[...], w2_ref[...])


def proj2_residual(x, a1, a2, w, layer, *, tm, tn, name):
    m, n = x.shape
    k1, k2 = a1.shape[1], a2.shape[1]
    assert k1 == k2
    tm = min(tm, m)
    return pl.pallas_call(
        _proj2_residual_kernel,
        grid=(m // tm, n // tn),
        in_specs=[
            pl.BlockSpec((tm, tn), lambda i, j: (i, j)),
            pl.BlockSpec((tm, k1), lambda i, j: (i, 0)),
            pl.BlockSpec((tm, k2), lambda i, j: (i, 0)),
            pl.BlockSpec((None, k1, tn), lambda i, j: (layer, 0, j)),
            pl.BlockSpec((None, k2, tn), lambda i, j: (layer, 1, j)),
        ],
        out_specs=pl.BlockSpec((tm, tn), lambda i, j: (i, j)),
        out_shape=jax.ShapeDtypeStruct((m, n), F32),
        compiler_params=_params(2), name=name,
    )(x, a1, a2, w, w)


def _xattn_kernel(x_ref, g_ref, wq_ref, k_ref, v_ref, wo_ref, gn_ref, o_ref, hn_ref, h_s, a_s, *, tm):
    _rmsnorm_into(h_s, 0, x_ref, g_ref[...], tm)
    q = _dot(h_s[...], wq_ref[...]).astype(BF16)
    scale = HEAD_DIM ** -0.5
    for hd in range(X_HEADS):
        cols = slice(hd * HEAD_DIM, (hd + 1) * HEAD_DIM)
        s = _dot_nt(q[:, cols], k_ref[:, cols]) * scale
        p = jnp.exp(s - jnp.max(s, axis=-1, keepdims=True))
        denom = jnp.sum(p, axis=-1, keepdims=True)
        a_s[:, cols] = (_dot(p.astype(BF16), v_ref[:, cols]) / denom).astype(BF16)
    o_ref[...] = x_ref[...] + _dot(a_s[...], wo_ref[...])
    _rmsnorm_into(hn_ref, 0, o_ref, gn_ref[...], tm)


def cross_attention_block(x, g, wq, kv, wo, g_next, layer, seq, n_mem, *, tm, name):
    m, d = x.shape
    tm = min(tm, seq)
    per_seq = seq // tm
    gain = pl.BlockSpec((None, 1, d), lambda i: (layer, 0, 0))
    rows = pl.BlockSpec((tm, d), lambda i: (i, 0))
    return pl.pallas_call(
        functools.partial(_xattn_kernel, tm=tm),
        grid=(m // tm,),
        in_specs=[
            rows, gain,
            pl.BlockSpec((None, d, X_WIDTH), lambda i: (layer, 0, 0)),
            pl.BlockSpec((n_mem, X_WIDTH), lambda i: (i // per_seq, 0)),
            pl.BlockSpec((n_mem, X_WIDTH), lambda i: (i // per_seq, 1)),
            pl.BlockSpec((None, X_WIDTH, d), lambda i: (layer, 0, 0)),
            gain,
        ],
        out_specs=[rows, rows],
        out_shape=[jax.ShapeDtypeStruct((m, d), F32), jax.ShapeDtypeStruct((m, d), BF16)],
        scratch_shapes=[pltpu.VMEM((tm, d), BF16), pltpu.VMEM((tm, X_WIDTH), BF16)],
        compiler_params=_params(1), name=name,
    )(x, g, wq, kv, kv, wo, g_next)


def _ffn_up_kernel(h_ref, halo_ref, wg_ref, wu_ref, cg_ref, cu_ref, bg_ref, bu_ref,
                   o_ref, h_s, *, tm, tiles_per_seq):
    i = pl.program_id(0)

    @pl.when(pl.program_id(1) == 0)
    def _():
        keep = jnp.where(i % tiles_per_seq == 0, 0.0, 1.0).astype(h_s.dtype)
        h_s[:HALO_ROWS, :] = halo_ref[...] * keep
        step = min(COPY_ROWS, tm)

        def body(r, carry):
            src = pl.ds(pl.multiple_of(r * step, step), step)
            dst = pl.ds(pl.multiple_of(HALO_ROWS + r * step, HALO_ROWS), step)
            h_s[dst, :] = h_ref[src, :]
            return carry

        lax.fori_loop(0, tm // step, body, 0)

    def conv(y, cw_ref, b_ref):
        cw = cw_ref[...]
        out = cw[FFN_CONV - 1:FFN_CONV, :] * y[HALO_ROWS:, :]
        for tap in range(FFN_CONV - 1):
            shift = FFN_CONV - 1 - tap
            out = out + cw[tap:tap + 1, :] * y[HALO_ROWS - shift:HALO_ROWS - shift + tm, :]
        return out + b_ref[...]

    h = h_s[...]
    gate = conv(_dot(h, wg_ref[...]), cg_ref, bg_ref)
    up = conv(_dot(h, wu_ref[...]), cu_ref, bu_ref)
    o_ref[...] = (_silu(gate) * up).astype(o_ref.dtype)


def ffn_up(h, w_up, conv_w, conv_b, layer, seq, *, tm, tn, name):
    m, d = h.shape
    d_ff = w_up.shape[-1] // 2
    tm = min(tm, seq)
    assert d_ff % tn == 0 and seq % tm == 0 and tm % HALO_ROWS == 0
    nj = d_ff // tn
    halo_blocks = tm // HALO_ROWS
    return pl.pallas_call(
        functools.partial(_ffn_up_kernel, tm=tm, tiles_per_seq=seq // tm),
        grid=(m // tm, nj),
        in_specs=[
            pl.BlockSpec((tm, d), lambda i, j: (i, 0)),
            pl.BlockSpec((HALO_ROWS, d), lambda i, j: (jnp.maximum(i * halo_blocks - 1, 0), 0)),
            pl.BlockSpec((None, d, tn), lambda i, j: (layer, 0, j)),
            pl.BlockSpec((None, d, tn), lambda i, j: (layer, 0, nj + j)),
            pl.BlockSpec((None, FFN_CONV, tn), lambda i, j: (layer, 0, j)),
            pl.BlockSpec((None, FFN_CONV, tn), lambda i, j: (layer, 0, nj + j)),
            pl.BlockSpec((None, 1, tn), lambda i, j: (layer, 0, j)),
            pl.BlockSpec((None, 1, tn), lambda i, j: (layer, 0, nj + j)),
        ],
        out_specs=pl.BlockSpec((tm, tn), lambda i, j: (i, j)),
        out_shape=jax.ShapeDtypeStruct((m, d_ff), BF16),
        scratch_shapes=[pltpu.VMEM((tm + HALO_ROWS, d), BF16)],
        compiler_params=_params(2), name=name,
    )(h, h, w_up, w_up, conv_w, conv_w, conv_b, conv_b)


def _matmul_residual_kernel(x_ref, a_ref, w_ref, o_ref):
    o_ref[...] = x_ref[...] + _dot(a_ref[...], w_ref[...])


def matmul_residual(x, a, w, layer, *, tm, tn, name):
    m, n = x.shape
    k = a.shape[1]
    tm = min(tm, m)
    return pl.pallas_call(
        _matmul_residual_kernel,
        grid=(m // tm, n // tn),
        in_specs=[
            pl.BlockSpec((tm, tn), lambda i, j: (i, j)),
            pl.BlockSpec((tm, k), lambda i, j: (i, 0)),
            pl.BlockSpec((None, k, tn), lambda i, j: (layer, 0, j)),
        ],
        out_specs=pl.BlockSpec((tm, tn), lambda i, j: (i, j)),
        out_shape=jax.ShapeDtypeStruct((m, n), F32),
        compiler_params=_params(2), name=name,
    )(x, a, w)


def _final_norm_kernel(x_ref, g_ref, o_ref, *, tm):
    _rmsnorm_into(o_ref, 0, x_ref, g_ref[...], tm)


def final_norm(x, g, *, tm, name):
    m, d = x.shape
    tm = min(tm, m)
    return pl.pallas_call(
        functools.partial(_final_norm_kernel, tm=tm),
        grid=(m // tm,),
        in_specs=[pl.BlockSpec((tm, d), lambda i: (i, 0)), pl.BlockSpec((1, d), lambda i: (0, 0))],
        out_specs=pl.BlockSpec((tm, d), lambda i: (i, 0)),
        out_shape=jax.ShapeDtypeStruct((m, d), F32),
        compiler_params=_params(1), name=name,
    )(x, g)


def kernel(x, mem, mix_norm, w_in, gdn_conv, gdn_a_log, gdn_dt_bias, gdn_norm, w_out, xattn_norm,
           mem_norm, w_xq, w_xkv, w_xo, ffn_norm, w_up, ffn_conv, ffn_conv_bias, w_down, final_norm_g):
    batch, seq, d_model = x.shape
    n_mem = mem.shape[1]
    depth = w_in.shape[0]
    sup = min(GDN_SUPER, seq)
    n_main = 3 * SB_WIDTH + 4 * GDN_WIDTH
    n_gate = 2 * GDN_HEADS

    w_gate = jnp.pad(w_in[:, :, n_main:], ((0, 0), (0, 0), (0, HEAD_DIM - n_gate)))
    row3 = lambda a: a.reshape(depth, 1, a.shape[-1])
    mix_g, xat_g, mem_g, ffn_g = row3(mix_norm), row3(xattn_norm), row3(mem_norm), row3(ffn_norm)
    gdn_g = row3(gdn_norm)
    conv_b = row3(ffn_conv_bias)

    xf = x.reshape(batch * seq, d_model)
    memf = mem.reshape(batch * n_mem, d_model)

    for l in range(depth):
        sb_qkv, gdn_qkvz, gates = in_projection(xf, mix_g, w_in, w_gate, l, 3 * SB_WIDTH, 4 * GDN_WIDTH,
                                                tm=1024, tn=512, name=f"inproj_{l}")
        sb_out = sb_attention(sb_qkv, batch, seq, name=f"sb_attn_{l}")
        ba = gates[:, :n_gate].reshape(batch, seq, n_gate).transpose(0, 2, 1)
        ba_rows = ba.reshape(batch, n_gate, seq // sup, sup)
        ba_cols = ba_rows.transpose(0, 1, 3, 2)
        gdn_out = gated_deltanet(gdn_qkvz, ba_rows, ba_cols, gdn_conv, gdn_a_log, gdn_dt_bias, gdn_g,
                                 l, batch, seq, name=f"gdn_{l}")
        xf = proj2_residual(xf, sb_out, gdn_out, w_out, l, tm=1024, tn=1024, name=f"outproj_{l}")

        kv = norm_matmul(memf, mem_g, w_xkv, l, BF16, tm=512, tn=512, name=f"mem_kv_{l}")
        xf, h_ffn = cross_attention_block(xf, xat_g, w_xq, kv, w_xo, ffn_g, l, seq, n_mem, tm=512,
                                          name=f"xattn_{l}")

        act = ffn_up(h_ffn, w_up, ffn_conv, conv_b, l, seq, tm=1024, tn=512, name=f"ffn_up_{l}")
        xf = matmul_residual(xf, act, w_down, l, tm=1024, tn=256, name=f"ffn_down_{l}")

    out = final_norm(xf, final_norm_g.reshape(1, d_model), tm=512, name="final_norm")
    return out.reshape(batch, seq, d_model)
```

```python
import functools
import math

import jax
import jax.numpy as jnp
from jax import lax
from jax.experimental import pallas as pl
from jax.experimental.pallas import tpu as pltpu

F32 = jnp.float32
BF16 = jnp.bfloat16

HEAD_DIM = 128
SB_HEADS = 8
SB_GROUP = 4
GDN_HEADS = 8
SB_WIDTH = SB_HEADS * HEAD_DIM
GDN_WIDTH = GDN_HEADS * HEAD_DIM
SHORT_CONV = 4
CHUNK = 64
GDN_SUPER = 256
GDN_GROUP = 4
X_HEADS = 4
X_WIDTH = X_HEADS * HEAD_DIM
FFN_CONV = 3
EPS = 1e-6

V7X_VMEM_BYTES = 64 * 1024 * 1024
VMEM_LIMIT_BYTES = V7X_VMEM_BYTES - 8 * 1024 * 1024
NORM_ROWS = 64
HALO_ROWS = 16
COPY_ROWS = 128


def _params(n_axes):
    return pltpu.CompilerParams(
        dimension_semantics=("arbitrary",) * n_axes, vmem_limit_bytes=VMEM_LIMIT_BYTES)


def _rmsnorm(x, g):
    ms = jnp.mean(x * x, axis=-1, keepdims=True)
    return x * lax.rsqrt(ms + EPS) * g


def _rmsnorm_into(dst_ref, dst_row0, x_ref, g, n_rows, scale=None):
    step = min(NORM_ROWS, n_rows)
    dst_align = math.gcd(step, dst_row0) if dst_row0 else step

    def body(r, carry):
        src = pl.ds(pl.multiple_of(r * step, step), step)
        dst = pl.ds(pl.multiple_of(dst_row0 + r * step, dst_align), step)
        y = _rmsnorm(x_ref[src, :], g)
        if scale is not None:
            y = y * scale
        dst_ref[dst, :] = y.astype(dst_ref.dtype)
        return carry

    lax.fori_loop(0, n_rows // step, body, 0)


def _aligned(start, multiple):
    return start if isinstance(start, int) else pl.multiple_of(start, multiple)


def _interleave(*gens):
    live = list(gens)
    while live:
        for g in list(live):
            if next(g, live) is live:
                live.remove(g)


def _sigmoid(x):
    return 1.0 / (1.0 + jnp.exp(-x))


def _silu(x):
    return x * _sigmoid(x)


def _softplus(x):
    return jnp.maximum(x, 0.0) + jnp.log1p(jnp.exp(-jnp.abs(x)))


def _dot(a, b):
    return lax.dot_general(a, b, (((1,), (0,)), ((), ())), preferred_element_type=F32)


def _dot_nt(a, b):
    return lax.dot_general(a, b, (((1,), (1,)), ((), ())), preferred_element_type=F32)


def _dot_tn(a, b):
    return lax.dot_general(a, b, (((0,), (0,)), ((), ())), preferred_element_type=F32)


def _norm_matmul_kernel(x_ref, g_ref, w_ref, o_ref, h_ref, *, tm):
    @pl.when(pl.program_id(1) == 0)
    def _():
        _rmsnorm_into(h_ref, 0, x_ref, g_ref[...], tm)

    o_ref[...] = _dot(h_ref[...], w_ref[...]).astype(o_ref.dtype)


def norm_matmul(x, g, w, layer, out_dtype, *, tm, tn, name):
    m, k = x.shape
    n = w.shape[-1]
    tm = min(tm, m)
    assert m % tm == 0 and n % tn == 0
    return pl.pallas_call(
        functools.partial(_norm_matmul_kernel, tm=tm),
        grid=(m // tm, n // tn),
        in_specs=[
            pl.BlockSpec((tm, k), lambda i, j: (i, 0)),
            pl.BlockSpec((None, 1, k), lambda i, j: (layer, 0, 0)),
            pl.BlockSpec((None, k, tn), lambda i, j: (layer, 0, j)),
        ],
        out_specs=pl.BlockSpec((tm, tn), lambda i, j: (i, j)),
        out_shape=jax.ShapeDtypeStruct((m, n), out_dtype),
        scratch_shapes=[pltpu.VMEM((tm, k), BF16)],
        compiler_params=_params(2), name=name,
    )(x, g, w)


def _in_proj_kernel(x_ref, g_ref, w_ref, ws_ref, osb_ref, ogdn_ref, ogate_ref, h_ref, *, tm, n_sb, n_gate):
    j = pl.program_id(1)

    @pl.when(j == 0)
    def _():
        _rmsnorm_into(h_ref, 0, x_ref, g_ref[...], tm)
        lane = lax.broadcasted_iota(jnp.int32, ws_ref.shape, 1)
        ogate_ref[...] = _dot(h_ref[...], jnp.where(lane < n_gate, ws_ref[...], 0.0))

    @pl.when(j < n_sb)
    def _():
        osb_ref[...] = _dot(h_ref[...], w_ref[...]).astype(osb_ref.dtype)

    @pl.when(j >= n_sb)
    def _():
        ogdn_ref[...] = _dot(h_ref[...], w_ref[...])


def in_projection(x, g, w, layer, n_sb_cols, n_gdn_cols, *, tm, tn, name):
    m, k = x.shape
    tm = min(tm, m)
    ns = HEAD_DIM
    n_main = n_sb_cols + n_gdn_cols
    n_gate = w.shape[-1] - n_main
    assert m % tm == 0 and n_sb_cols % tn == 0 and n_gdn_cols % tn == 0
    assert n_main % ns == 0 and 0 < n_gate <= ns
    n_sb, n_gdn = n_sb_cols // tn, n_gdn_cols // tn
    return pl.pallas_call(
        functools.partial(_in_proj_kernel, tm=tm, n_sb=n_sb, n_gate=n_gate),
        grid=(m // tm, n_sb + n_gdn),
        in_specs=[
            pl.BlockSpec((tm, k), lambda i, j: (i, 0), pipeline_mode=pl.Buffered(1)),
            pl.BlockSpec((None, 1, k), lambda i, j: (layer, 0, 0)),
            pl.BlockSpec((None, k, tn), lambda i, j: (layer, 0, j)),
            pl.BlockSpec((None, k, ns), lambda i, j: (layer, 0, n_main // ns)),
        ],
        out_specs=[
            pl.BlockSpec((tm, tn), lambda i, j: (i, jnp.minimum(j, n_sb - 1))),
            pl.BlockSpec((tm, tn), lambda i, j: (i, jnp.maximum(j - n_sb, 0))),
            pl.BlockSpec((tm, ns), lambda i, j: (i, 0)),
        ],
        out_shape=[jax.ShapeDtypeStruct((m, n_sb_cols), BF16),
                   jax.ShapeDtypeStruct((m, n_gdn_cols), F32),
                   jax.ShapeDtypeStruct((m, ns), F32)],
        scratch_shapes=[pltpu.VMEM((tm, k), BF16)],
        compiler_params=_params(2), name=name,
    )(x, g, w, w)


def _sb_attn_kernel(q_ref, k_ref, v_ref, o_ref, acc_s, carry_s, *, seq, tile, sub):
    scale = HEAD_DIM ** -0.5
    n_sub = tile // sub
    r = lax.broadcasted_iota(jnp.int32, (sub, sub), 0)
    c = lax.broadcasted_iota(jnp.int32, (sub, sub), 1)
    suffix = jnp.where(r > c, 1.0, 0.0).astype(BF16)
    row = lax.broadcasted_iota(jnp.int32, (tile, tile), 0)
    col = lax.broadcasted_iota(jnp.int32, (tile, tile), 1)
    causal = col < row
    sign_bit = jnp.uint32(0x80000000)

    heads = range(SB_GROUP)
    lanes = lambda hh: slice(hh * HEAD_DIM, (hh + 1) * HEAD_DIM)

    def block_rounds(hh, q, kb, first):
        rows = pl.ds(pl.multiple_of(kb * tile, tile), tile)
        z = _dot_nt(q, k_ref[rows, lanes(hh)]) * scale
        yield
        neg_abs = lax.bitcast_convert_type(lax.bitcast_convert_type(z, jnp.uint32) | sign_bit, F32)
        log_beta = jnp.minimum(z, 0.0) - jnp.log(1.0 + jnp.exp(neg_abs))
        log_stay = log_beta - z
        if first:
            log_stay = jnp.where(causal, log_stay, 0.0)
        laters = [None] * n_sub
        run = None if first else carry_s[hh]
        for s in reversed(range(n_sub)):
            ls = log_stay[:, s * sub:(s + 1) * sub]
            cum = _dot(ls.astype(BF16), suffix)
            laters[s] = cum if run is None else cum + run
            tot = jnp.sum(ls, axis=-1, keepdims=True)
            run = tot if run is None else run + tot
        carry_s[hh] = run
        yield
        w = jnp.exp(log_beta + jnp.concatenate(laters, axis=1))
        if first:
            w = jnp.where(causal, w, 0.0)
        pv = _dot(w.astype(BF16), v_ref[rows, lanes(hh)])
        if first:
            acc_s[hh] = pv
        else:
            acc_s[hh] += pv
        yield

    def q_block(qi, c):
        qrows = pl.ds(pl.multiple_of(qi * tile, tile), tile)
        qs = [q_ref[qrows, lanes(hh)] for hh in heads]
        _interleave(*[block_rounds(hh, qs[hh], qi, True) for hh in heads])

        def k_step(s, c2):
            _interleave(*[block_rounds(hh, qs[hh], qi - 1 - s, False) for hh in heads])
            return c2

        lax.fori_loop(0, qi, k_step, 0)
        for hh in heads:
            o_ref[qrows, lanes(hh)] = acc_s[hh].astype(o_ref.dtype)
        return c

    lax.fori_loop(0, seq // tile, q_block, 0)


def sb_attention(qkv, batch, seq, *, tile=512, sub=256, name):
    tile = min(tile, seq)
    sub = min(sub, tile)
    n_groups = SB_HEADS // SB_GROUP
    width = SB_GROUP * HEAD_DIM
    spec = lambda part: pl.BlockSpec((seq, width), lambda b, g: (b, part * n_groups + g))
    return pl.pallas_call(
        functools.partial(_sb_attn_kernel, seq=seq, tile=tile, sub=sub),
        grid=(batch, n_groups),
        in_specs=[spec(0), spec(1), spec(2)],
        out_specs=pl.BlockSpec((seq, width), lambda b, g: (b, g)),
        out_shape=jax.ShapeDtypeStruct((batch * seq, SB_WIDTH), BF16),
        scratch_shapes=[pltpu.VMEM((SB_GROUP, tile, HEAD_DIM), F32), pltpu.VMEM((SB_GROUP, tile, 1), F32)],
        compiler_params=_params(2), name=name,
    )(qkv, qkv, qkv)


def _gdn_kernel(alog_ref, dtb_ref, xq_ref, xk_ref, xv_ref, z_ref, cwq_ref, cwk_ref, cwv_ref,
                brow_ref, arow_ref, bcol_ref, acol_ref, ng_ref, o_ref,
                u_s, wq_s, kt_s, at_s, gcr_s, gcol_s, *, layer, seq):
    sup = min(GDN_SUPER, seq)
    per = sup // CHUNK
    n_sup = seq // sup
    group = pl.program_id(1)
    lanes = lambda hh: slice(hh * HEAD_DIM, (hh + 1) * HEAD_DIM)

    ri = lax.broadcasted_iota(jnp.int32, (sup, sup), 0)
    ci = lax.broadcasted_iota(jnp.int32, (sup, sup), 1)
    same = (ri // CHUNK) == (ci // CHUNK)
    in_lower = jnp.logical_and(same, ri >= ci)
    neg_inf_mask = jnp.where(in_lower, 0.0, -jnp.inf)
    neg_strict = jnp.where(jnp.logical_and(same, ri > ci), -1.0, 0.0)
    tri_lower = jnp.where(in_lower, 1.0, 0.0)
    tri_upper = jnp.where(jnp.logical_and(same, ri <= ci), 1.0, 0.0)
    block_ones = jnp.where(same, 1.0, 0.0)
    sup_lane = lax.broadcasted_iota(jnp.int32, (sup, n_sup), 1)

    def l2norm(y):
        return y * lax.rsqrt(jnp.sum(y * y, axis=-1, keepdims=True) + EPS)

    def conv_silu(x_ref, cw_ref, hh, sc):
        cw = cw_ref[:, lanes(hh)]
        xb = x_ref[pl.ds(_aligned(sc * sup, sup), sup), lanes(hh)]
        if isinstance(sc, int) and sc == 0:
            prev = jnp.zeros((8, HEAD_DIM), F32)
        else:
            prev = x_ref[pl.ds(pl.multiple_of(jnp.maximum(sc * sup - 8, 0), 8), 8), lanes(hh)]
            prev = jnp.where(sc > 0, prev, 0.0)
        xx = jnp.concatenate([prev, xb], axis=0)
        y = cw[SHORT_CONV - 1:SHORT_CONV, :] * xx[8:, :]
        for tap in range(SHORT_CONV - 1):
            shift = SHORT_CONV - 1 - tap
            y = y + cw[tap:tap + 1, :] * xx[8 - shift:8 - shift + sup, :]
        return _silu(y)

    hdot = functools.partial(jnp.dot, precision=lax.Precision.HIGHEST, preferred_element_type=F32)
    for hh in range(GDN_GROUP):
        head = group * GDN_GROUP + hh
        neg_a = -jnp.exp(jnp.full((1, 1), alog_ref[layer, head], F32))
        dtb = dtb_ref[layer, head]
        g_row = neg_a * _softplus(arow_ref[hh] + dtb)
        g_col = neg_a * _softplus(acol_ref[hh] + dtb)
        gcr_s[hh] = hdot(g_row, tri_upper)
        gcol_s[hh, 0] = _sigmoid(bcol_ref[hh])
        gcol_s[hh, 1] = hdot(tri_lower, g_col)
        gcol_s[hh, 2] = hdot(block_ones, g_col)

    heads = range(GDN_GROUP)
    n_sq = max(CHUNK.bit_length() - 2, 0)

    def prep_rounds(sc, slot):
        st = [dict() for _ in heads]
        for hh in heads:
            s = st[hh]
            sel = sup_lane == sc
            pick = lambda kind: jnp.sum(jnp.where(sel, gcol_s[hh, kind], 0.0), axis=1, keepdims=True)
            bcol, gcol, glast = pick(0), pick(1), pick(2)
            grow = gcr_s[hh, pl.ds(sc, 1), :]
            qc = l2norm(conv_silu(xq_ref, cwq_ref, hh, sc)) * (HEAD_DIM ** -0.5)
            kc = l2norm(conv_silu(xk_ref, cwk_ref, hh, sc))
            s["decay"] = jnp.exp((gcol - grow) + neg_inf_mask)
            kb = kc * bcol
            egc = jnp.exp(gcol)
            s["rhs"] = jnp.concatenate([conv_silu(xv_ref, cwv_ref, hh, sc) * bcol, kb * egc], axis=1)
            s["kb"], s["kc"], s["qc"] = kb.astype(BF16), kc.astype(BF16), qc.astype(BF16)
            qd = (qc * egc).astype(BF16)
            for p in range(per):
                blk = slice(p * CHUNK, (p + 1) * CHUNK)
                wq_s[slot, hh, p, CHUNK:, :] = qd[blk, :]
            kt_s[slot, hh] = (kc * jnp.exp(glast - gcol)).astype(BF16)
        yield
        for hh in heads:
            s = st[hh]
            s["y"] = _dot_nt(s["kb"], s["kc"]) * s["decay"] * neg_strict
            s["z"] = s["y"]
            attn = (_dot_nt(s["qc"], s["kc"]) * s["decay"]).astype(BF16)
            for p in range(per):
                blk = slice(p * CHUNK, (p + 1) * CHUNK)
                at_s[slot, hh, blk, :] = attn[blk, blk]
        yield
        for m in range(n_sq):
            for hh in heads:
                s = st[hh]
                yb = s["y"].astype(BF16)
                if m > 0:
                    s["z"] = s["z"] + s["y"] + _dot(s["z"].astype(BF16), yb)
                s["y"] = _dot(yb, yb)
            yield
        for hh in heads:
            s = st[hh]
            s["z"] = s["z"] + s["y"] + _dot(s["z"].astype(BF16), s["y"].astype(BF16))
        yield
        for hh in heads:
            s = st[hh]
            uw = s["rhs"] + _dot(s["z"].astype(BF16), s["rhs"].astype(BF16))
            u_s[slot, hh] = uw[:, :HEAD_DIM]
            w = uw[:, HEAD_DIM:].astype(BF16)
            for p in range(per):
                blk = slice(p * CHUNK, (p + 1) * CHUNK)
                wq_s[slot, hh, p, :CHUNK, :] = w[blk, :]
        yield

    ng = ng_ref[...]

    def scan_rounds(sc, slot, states):
        chunk_decay = [jnp.exp(gcr_s[hh, pl.ds(sc, 1), :]) for hh in heads]
        for p in range(per):
            blk = slice(p * CHUNK, (p + 1) * CHUNK)
            rows = pl.ds(_aligned(sc * sup + p * CHUNK, CHUNK), CHUNK)
            vnb, qs = [None] * GDN_GROUP, [None] * GDN_GROUP
            for hh in heads:
                both = _dot(wq_s[slot, hh, p], states[hh].astype(BF16))
                vnb[hh] = (u_s[slot, hh, blk, :] - both[:CHUNK, :]).astype(BF16)
                qs[hh] = both[CHUNK:, :]
            yield
            for hh in heads:
                o = qs[hh] + _dot(at_s[slot, hh, blk, :], vnb[hh])
                last = p * CHUNK + CHUNK - 1
                cd = chunk_decay[hh][:, last:last + 1]
                states[hh] = states[hh] * cd + _dot_tn(kt_s[slot, hh, blk, :], vnb[hh])
                o_ref[rows, lanes(hh)] = (_rmsnorm(o, ng) * _silu(z_ref[rows, lanes(hh)])).astype(o_ref.dtype)
            yield

    def step(i, states):
        states = list(states)
        slot = lax.rem(i, 2)
        _interleave(prep_rounds(i, slot), scan_rounds(i - 1, 1 - slot, states))
        return tuple(states)

    _interleave(prep_rounds(0, 0))
    states = tuple(jnp.zeros((HEAD_DIM, HEAD_DIM), F32) for _ in heads)
    states = list(lax.fori_loop(1, n_sup, step, states))
    _interleave(scan_rounds(n_sup - 1, (n_sup - 1) % 2, states))


def gated_deltanet(qkvz, ba_rows, ba_cols, conv_w, a_log, dt_bias, norm_g, layer, batch, seq, *, name):
    sup = min(GDN_SUPER, seq)
    n_sup = seq // sup
    n_groups = GDN_HEADS // GDN_GROUP
    width = GDN_GROUP * HEAD_DIM
    col = lambda part: pl.BlockSpec((seq, width), lambda b, g: (b, part * n_groups + g))
    cw = lambda part: pl.BlockSpec((None, SHORT_CONV, width), lambda b, g: (layer, 0, part * n_groups + g))
    rows = lambda part: pl.BlockSpec((None, GDN_GROUP, n_sup, sup), lambda b, g: (b, part * n_groups + g, 0, 0))
    cols = lambda part: pl.BlockSpec((None, GDN_GROUP, sup, n_sup), lambda b, g: (b, part * n_groups + g, 0, 0))
    smem = pl.BlockSpec(memory_space=pltpu.SMEM)
    per_head = lambda shape, dt: pltpu.VMEM((GDN_GROUP,) + shape, dt)
    slots = lambda shape, dt: pltpu.VMEM((2, GDN_GROUP) + shape, dt)
    return pl.pallas_call(
        functools.partial(_gdn_kernel, layer=layer, seq=seq),
        grid=(batch, n_groups),
        in_specs=[smem, smem, col(0), col(1), col(2), col(3), cw(0), cw(1), cw(2),
                  rows(0), rows(1), cols(0), cols(1),
                  pl.BlockSpec((None, 1, HEAD_DIM), lambda b, g: (layer, 0, 0))],
        out_specs=pl.BlockSpec((seq, width), lambda b, g: (b, g)),
        out_shape=jax.ShapeDtypeStruct((batch * seq, GDN_WIDTH), BF16),
        scratch_shapes=[
            slots((sup, HEAD_DIM), F32),
            slots((sup // CHUNK, 2 * CHUNK, HEAD_DIM), BF16),
            slots((sup, HEAD_DIM), BF16),
            slots((sup, CHUNK), BF16),
            per_head((n_sup, sup), F32),
            per_head((3, sup, n_sup), F32),
        ],
        compiler_params=_params(2), name=name,
    )(a_log, dt_bias, qkvz, qkvz, qkvz, qkvz, conv_w, conv_w, conv_w,
      ba_rows, ba_rows, ba_cols, ba_cols, norm_g)


def _proj2_residual_kernel(x_ref, a1_ref, a2_ref, w1_ref, w2_ref, o_ref):
    o_ref[...] = x_ref[...] + _dot(a1_ref[...], w1_ref[...]) + _dot(a2_ref[...], w2_ref[...])


def proj2_residual(x, a1, a2, w, layer, *, tm, tn, name):
    m, n = x.shape
    k1, k2 = a1.shape[1], a2.shape[1]
    assert k1 == k2
    tm = min(tm, m)
    return pl.pallas_call(
        _proj2_residual_kernel,
        grid=(m // tm, n // tn),
        in_specs=[
            pl.BlockSpec((tm, tn), lambda i, j: (i, j)),
            pl.BlockSpec((tm, k1), lambda i, j: (i, 0)),
            pl.BlockSpec((tm, k2), lambda i, j: (i, 0)),
            pl.BlockSpec((None, k1, tn), lambda i, j: (layer, 0, j)),
            pl.BlockSpec((None, k2, tn), lambda i, j: (layer, 1, j)),
        ],
        out_specs=pl.BlockSpec((tm, tn), lambda i, j: (i, j)),
        out_shape=jax.ShapeDtypeStruct((m, n), F32),
        compiler_params=_params(2), name=name,
    )(x, a1, a2, w, w)


def _xattn_kernel(x_ref, g_ref, wq_ref, k_ref, v_ref, wo_ref, gn_ref, o_ref, hn_ref, h_s, a_s, *, tm):
    _rmsnorm_into(h_s, 0, x_ref, g_ref[...], tm)
    q = _dot(h_s[...], wq_ref[...]).astype(BF16)
    scale = HEAD_DIM ** -0.5
    for hd in range(X_HEADS):
        cols = slice(hd * HEAD_DIM, (hd + 1) * HEAD_DIM)
        s = _dot_nt(q[:, cols], k_ref[:, cols]) * scale
        p = jnp.exp(s - jnp.max(s, axis=-1, keepdims=True))
        denom = jnp.sum(p, axis=-1, keepdims=True)
        a_s[:, cols] = (_dot(p.astype(BF16), v_ref[:, cols]) / denom).astype(BF16)
    o_ref[...] = x_ref[...] + _dot(a_s[...], wo_ref[...])
    _rmsnorm_into(hn_ref, 0, o_ref, gn_ref[...], tm)


def cross_attention_block(x, g, wq, kv, wo, g_next, layer, seq, n_mem, *, tm, name):
    m, d = x.shape
    tm = min(tm, seq)
    per_seq = seq // tm
    gain = pl.BlockSpec((None, 1, d), lambda i: (layer, 0, 0))
    rows = pl.BlockSpec((tm, d), lambda i: (i, 0))
    return pl.pallas_call(
        functools.partial(_xattn_kernel, tm=tm),
        grid=(m // tm,),
        in_specs=[
            rows, gain,
            pl.BlockSpec((None, d, X_WIDTH), lambda i: (layer, 0, 0)),
            pl.BlockSpec((n_mem, X_WIDTH), lambda i: (i // per_seq, 0)),
            pl.BlockSpec((n_mem, X_WIDTH), lambda i: (i // per_seq, 1)),
            pl.BlockSpec((None, X_WIDTH, d), lambda i: (layer, 0, 0)),
            gain,
        ],
        out_specs=[rows, rows],
        out_shape=[jax.ShapeDtypeStruct((m, d), F32), jax.ShapeDtypeStruct((m, d), BF16)],
        scratch_shapes=[pltpu.VMEM((tm, d), BF16), pltpu.VMEM((tm, X_WIDTH), BF16)],
        compiler_params=_params(1), name=name,
    )(x, g, wq, kv, kv, wo, g_next)


def _ffn_up_kernel(h_ref, halo_ref, wg_ref, wu_ref, cg_ref, cu_ref, bg_ref, bu_ref,
                   o_ref, h_s, *, tm, tiles_per_seq):
    i = pl.program_id(0)

    @pl.when(pl.program_id(1) == 0)
    def _():
        keep = jnp.where(i % tiles_per_seq == 0, 0.0, 1.0).astype(h_s.dtype)
        h_s[:HALO_ROWS, :] = halo_ref[...] * keep
        step = min(COPY_ROWS, tm)

        def body(r, carry):
            src = pl.ds(pl.multiple_of(r * step, step), step)
            dst = pl.ds(pl.multiple_of(HALO_ROWS + r * step, HALO_ROWS), step)
            h_s[dst, :] = h_ref[src, :]
            return carry

        lax.fori_loop(0, tm // step, body, 0)

    def conv(y, cw_ref, b_ref):
        cw = cw_ref[...]
        out = cw[FFN_CONV - 1:FFN_CONV, :] * y[HALO_ROWS:, :]
        for tap in range(FFN_CONV - 1):
            shift = FFN_CONV - 1 - tap
            out = out + cw[tap:tap + 1, :] * y[HALO_ROWS - shift:HALO_ROWS - shift + tm, :]
        return out + b_ref[...]

    h = h_s[...]
    gate = conv(_dot(h, wg_ref[...]), cg_ref, bg_ref)
    up = conv(_dot(h, wu_ref[...]), cu_ref, bu_ref)
    o_ref[...] = (_silu(gate) * up).astype(o_ref.dtype)


def ffn_up(h, w_up, conv_w, conv_b, layer, seq, *, tm, tn, name):
    m, d = h.shape
    d_ff = w_up.shape[-1] // 2
    tm = min(tm, seq)
    assert d_ff % tn == 0 and seq % tm == 0 and tm % HALO_ROWS == 0
    nj = d_ff // tn
    halo_blocks = tm // HALO_ROWS
    return pl.pallas_call(
        functools.partial(_ffn_up_kernel, tm=tm, tiles_per_seq=seq // tm),
        grid=(m // tm, nj),
        in_specs=[
            pl.BlockSpec((tm, d), lambda i, j: (i, 0)),
            pl.BlockSpec((HALO_ROWS, d), lambda i, j: (jnp.maximum(i * halo_blocks - 1, 0), 0)),
            pl.BlockSpec((None, d, tn), lambda i, j: (layer, 0, j)),
            pl.BlockSpec((None, d, tn), lambda i, j: (layer, 0, nj + j)),
            pl.BlockSpec((None, FFN_CONV, tn), lambda i, j: (layer, 0, j)),
            pl.BlockSpec((None, FFN_CONV, tn), lambda i, j: (layer, 0, nj + j)),
            pl.BlockSpec((None, 1, tn), lambda i, j: (layer, 0, j)),
            pl.BlockSpec((None, 1, tn), lambda i, j: (layer, 0, nj + j)),
        ],
        out_specs=pl.BlockSpec((tm, tn), lambda i, j: (i, j)),
        out_shape=jax.ShapeDtypeStruct((m, d_ff), BF16),
        scratch_shapes=[pltpu.VMEM((tm + HALO_ROWS, d), BF16)],
        compiler_params=_params(2), name=name,
    )(h, h, w_up, w_up, conv_w, conv_w, conv_b, conv_b)


def _matmul_residual_kernel(x_ref, a_ref, w_ref, o_ref):
    o_ref[...] = x_ref[...] + _dot(a_ref[...], w_ref[...])


def matmul_residual(x, a, w, layer, *, tm, tn, name):
    m, n = x.shape
    k = a.shape[1]
    tm = min(tm, m)
    return pl.pallas_call(
        _matmul_residual_kernel,
        grid=(m // tm, n // tn),
        in_specs=[
            pl.BlockSpec((tm, tn), lambda i, j: (i, j)),
            pl.BlockSpec((tm, k), lambda i, j: (i, 0)),
            pl.BlockSpec((None, k, tn), lambda i, j: (layer, 0, j)),
        ],
        out_specs=pl.BlockSpec((tm, tn), lambda i, j: (i, j)),
        out_shape=jax.ShapeDtypeStruct((m, n), F32),
        compiler_params=_params(2), name=name,
    )(x, a, w)


def _final_norm_kernel(x_ref, g_ref, o_ref, *, tm):
    _rmsnorm_into(o_ref, 0, x_ref, g_ref[...], tm)


def final_norm(x, g, *, tm, name):
    m, d = x.shape
    tm = min(tm, m)
    return pl.pallas_call(
        functools.partial(_final_norm_kernel, tm=tm),
        grid=(m // tm,),
        in_specs=[pl.BlockSpec((tm, d), lambda i: (i, 0)), pl.BlockSpec((1, d), lambda i: (0, 0))],
        out_specs=pl.BlockSpec((tm, d), lambda i: (i, 0)),
        out_shape=jax.ShapeDtypeStruct((m, d), F32),
        compiler_params=_params(1), name=name,
    )(x, g)


def kernel(x, mem, mix_norm, w_in, gdn_conv, gdn_a_log, gdn_dt_bias, gdn_norm, w_out, xattn_norm,
           mem_norm, w_xq, w_xkv, w_xo, ffn_norm, w_up, ffn_conv, ffn_conv_bias, w_down, final_norm_g):
    batch, seq, d_model = x.shape
    n_mem = mem.shape[1]
    depth = w_in.shape[0]
    sup = min(GDN_SUPER, seq)
    n_gate = 2 * GDN_HEADS

    w_out_b = w_out.astype(BF16)
    w_down_b = w_down.astype(BF16)
    row3 = lambda a: a.reshape(depth, 1, a.shape[-1])
    mix_g, xat_g, mem_g, ffn_g = row3(mix_norm), row3(xattn_norm), row3(mem_norm), row3(ffn_norm)
    gdn_g = row3(gdn_norm)
    conv_b = row3(ffn_conv_bias)

    xf = x.reshape(batch * seq, d_model)
    memf = mem.reshape(batch * n_mem, d_model)

    for l in range(depth):
        sb_qkv, gdn_qkvz, gates = in_projection(xf, mix_g, w_in, l, 3 * SB_WIDTH, 4 * GDN_WIDTH,
                                                tm=2048, tn=512, name=f"inproj_{l}")
        sb_out = sb_attention(sb_qkv, batch, seq, name=f"sb_attn_{l}")
        ba = gates[:, :n_gate].reshape(batch, seq, n_gate).transpose(0, 2, 1)
        ba_rows = ba.reshape(batch, n_gate, seq // sup, sup)
        ba_cols = ba_rows.transpose(0, 1, 3, 2)
        gdn_out = gated_deltanet(gdn_qkvz, ba_rows, ba_cols, gdn_conv, gdn_a_log, gdn_dt_bias, gdn_g,
                                 l, batch, seq, name=f"gdn_{l}")
        xf = proj2_residual(xf, sb_out, gdn_out, w_out_b, l, tm=1024, tn=1024, name=f"outproj_{l}")

        kv = norm_matmul(memf, mem_g, w_xkv, l, BF16, tm=512, tn=512, name=f"mem_kv_{l}")
        xf, h_ffn = cross_attention_block(xf, xat_g, w_xq, kv, w_xo, ffn_g, l, seq, n_mem, tm=512,
                                          name=f"xattn_{l}")

        act = ffn_up(h_ffn, w_up, ffn_conv, conv_b, l, seq, tm=1024, tn=512, name=f"ffn_up_{l}")
        xf = matmul_residual(xf, act, w_down_b, l, tm=1024, tn=512, name=f"ffn_down_{l}")

    out = final_norm(xf, final_norm_g.reshape(1, d_model), tm=512, name="final_norm")
    return out.reshape(batch, seq, d_model)
```

```python
import functools
import math

import jax
import jax.numpy as jnp
from jax import lax
from jax.experimental import pallas as pl
from jax.experimental.pallas import tpu as pltpu

F32 = jnp.float32
BF16 = jnp.bfloat16

HEAD_DIM = 128
SB_HEADS = 8
SB_GROUP = 4
GDN_HEADS = 8
SB_WIDTH = SB_HEADS * HEAD_DIM
GDN_WIDTH = GDN_HEADS * HEAD_DIM
SHORT_CONV = 4
CHUNK = 64
GDN_SUPER = 256
GDN_GROUP = 4
X_HEADS = 4
X_WIDTH = X_HEADS * HEAD_DIM
FFN_CONV = 3
EPS = 1e-6

V7X_VMEM_BYTES = 64 * 1024 * 1024
VMEM_LIMIT_BYTES = V7X_VMEM_BYTES - 8 * 1024 * 1024
NORM_ROWS = 64
HALO_ROWS = 16
COPY_ROWS = 128


def _params(n_axes):
    return pltpu.CompilerParams(
        dimension_semantics=("arbitrary",) * n_axes, vmem_limit_bytes=VMEM_LIMIT_BYTES)


def _rmsnorm(x, g):
    ms = jnp.mean(x * x, axis=-1, keepdims=True)
    return x * lax.rsqrt(ms + EPS) * g


def _rmsnorm_into(dst_ref, dst_row0, x_ref, g, n_rows, scale=None):
    step = min(NORM_ROWS, n_rows)
    dst_align = math.gcd(step, dst_row0) if dst_row0 else step

    def body(r, carry):
        src = pl.ds(pl.multiple_of(r * step, step), step)
        dst = pl.ds(pl.multiple_of(dst_row0 + r * step, dst_align), step)
        y = _rmsnorm(x_ref[src, :], g)
        if scale is not None:
            y = y * scale
        dst_ref[dst, :] = y.astype(dst_ref.dtype)
        return carry

    lax.fori_loop(0, n_rows // step, body, 0)


def _aligned(start, multiple):
    return start if isinstance(start, int) else pl.multiple_of(start, multiple)


def _interleave(*gens):
    live = list(gens)
    while live:
        for g in list(live):
            if next(g, live) is live:
                live.remove(g)


def _sigmoid(x):
    return 1.0 / (1.0 + jnp.exp(-x))


def _silu(x):
    return x * _sigmoid(x)


def _softplus(x):
    return jnp.maximum(x, 0.0) + jnp.log1p(jnp.exp(-jnp.abs(x)))


def _dot(a, b):
    return lax.dot_general(a, b, (((1,), (0,)), ((), ())), preferred_element_type=F32)


def _dot_nt(a, b):
    return lax.dot_general(a, b, (((1,), (1,)), ((), ())), preferred_element_type=F32)


def _dot_tn(a, b):
    return lax.dot_general(a, b, (((0,), (0,)), ((), ())), preferred_element_type=F32)


def _norm_matmul_kernel(x_ref, g_ref, w_ref, o_ref, h_ref, *, tm):
    @pl.when(pl.program_id(1) == 0)
    def _():
        _rmsnorm_into(h_ref, 0, x_ref, g_ref[...], tm)

    o_ref[...] = _dot(h_ref[...], w_ref[...]).astype(o_ref.dtype)


def norm_matmul(x, g, w, layer, out_dtype, *, tm, tn, name):
    m, k = x.shape
    n = w.shape[-1]
    tm = min(tm, m)
    assert m % tm == 0 and n % tn == 0
    return pl.pallas_call(
        functools.partial(_norm_matmul_kernel, tm=tm),
        grid=(m // tm, n // tn),
        in_specs=[
            pl.BlockSpec((tm, k), lambda i, j: (i, 0)),
            pl.BlockSpec((None, 1, k), lambda i, j: (layer, 0, 0)),
            pl.BlockSpec((None, k, tn), lambda i, j: (layer, 0, j)),
        ],
        out_specs=pl.BlockSpec((tm, tn), lambda i, j: (i, j)),
        out_shape=jax.ShapeDtypeStruct((m, n), out_dtype),
        scratch_shapes=[pltpu.VMEM((tm, k), BF16)],
        compiler_params=_params(2), name=name,
    )(x, g, w)


def _in_proj_kernel(x_ref, g_ref, wt_ref, wst_ref, osb_ref, ogdn_ref, ogate_ref, h_ref, *, tm, n_sb):
    j = pl.program_id(1)

    @pl.when(j == 0)
    def _():
        _rmsnorm_into(h_ref, 0, x_ref, g_ref[...], tm)
        ogate_ref[...] = _dot_nt(h_ref[...], wst_ref[...])

    @pl.when(j < n_sb)
    def _():
        osb_ref[...] = _dot_nt(h_ref[...], wt_ref[...]).astype(osb_ref.dtype)

    @pl.when(j >= n_sb)
    def _():
        ogdn_ref[...] = _dot_nt(h_ref[...], wt_ref[...])


def in_projection(x, g, w_t, layer, n_sb_cols, n_gdn_cols, *, tm, tn, name):
    m, k = x.shape
    tm = min(tm, m)
    n_main = n_sb_cols + n_gdn_cols
    n_gate = w_t.shape[1] - n_main
    assert m % tm == 0 and n_sb_cols % tn == 0 and n_gdn_cols % tn == 0
    assert n_gate > 0 and n_main % n_gate == 0 and n_gate % 8 == 0
    n_sb, n_gdn = n_sb_cols // tn, n_gdn_cols // tn
    return pl.pallas_call(
        functools.partial(_in_proj_kernel, tm=tm, n_sb=n_sb),
        grid=(m // tm, n_sb + n_gdn),
        in_specs=[
            pl.BlockSpec((tm, k), lambda i, j: (i, 0), pipeline_mode=pl.Buffered(1)),
            pl.BlockSpec((None, 1, k), lambda i, j: (layer, 0, 0)),
            pl.BlockSpec((None, tn, k), lambda i, j: (layer, j, 0)),
            pl.BlockSpec((None, n_gate, k), lambda i, j: (layer, n_main // n_gate, 0)),
        ],
        out_specs=[
            pl.BlockSpec((tm, tn), lambda i, j: (i, jnp.minimum(j, n_sb - 1))),
            pl.BlockSpec((tm, tn), lambda i, j: (i, jnp.maximum(j - n_sb, 0))),
            pl.BlockSpec((tm, n_gate), lambda i, j: (i, 0)),
        ],
        out_shape=[jax.ShapeDtypeStruct((m, n_sb_cols), BF16),
                   jax.ShapeDtypeStruct((m, n_gdn_cols), F32),
                   jax.ShapeDtypeStruct((m, n_gate), F32)],
        scratch_shapes=[pltpu.VMEM((tm, k), BF16)],
        compiler_params=_params(2), name=name,
    )(x, g, w_t, w_t)


def _sb_attn_kernel(q_ref, k_ref, v_ref, o_ref, acc_s, carry_s, *, seq, tile, sub):
    scale = HEAD_DIM ** -0.5
    n_sub = tile // sub
    r = lax.broadcasted_iota(jnp.int32, (sub, sub), 0)
    c = lax.broadcasted_iota(jnp.int32, (sub, sub), 1)
    suffix = jnp.where(r > c, 1.0, 0.0).astype(BF16)
    row = lax.broadcasted_iota(jnp.int32, (tile, tile), 0)
    col = lax.broadcasted_iota(jnp.int32, (tile, tile), 1)
    causal = col < row
    sign_bit = jnp.uint32(0x80000000)

    heads = range(SB_GROUP)
    lanes = lambda hh: slice(hh * HEAD_DIM, (hh + 1) * HEAD_DIM)

    def block_rounds(hh, q, kb, first):
        rows = pl.ds(pl.multiple_of(kb * tile, tile), tile)
        z = _dot_nt(q, k_ref[rows, lanes(hh)]) * scale
        yield
        neg_abs = lax.bitcast_convert_type(lax.bitcast_convert_type(z, jnp.uint32) | sign_bit, F32)
        log_beta = jnp.minimum(z, 0.0) - jnp.log(1.0 + jnp.exp(neg_abs))
        log_stay = log_beta - z
        if first:
            log_stay = jnp.where(causal, log_stay, 0.0)
        laters = [None] * n_sub
        run = None if first else carry_s[hh]
        for s in reversed(range(n_sub)):
            ls = log_stay[:, s * sub:(s + 1) * sub]
            cum = _dot(ls.astype(BF16), suffix)
            laters[s] = cum if run is None else cum + run
            tot = jnp.sum(ls, axis=-1, keepdims=True)
            run = tot if run is None else run + tot
        carry_s[hh] = run
        yield
        w = jnp.exp(log_beta + jnp.concatenate(laters, axis=1))
        if first:
            w = jnp.where(causal, w, 0.0)
        pv = _dot(w.astype(BF16), v_ref[rows, lanes(hh)])
        if first:
            acc_s[hh] = pv
        else:
            acc_s[hh] += pv
        yield

    def q_block(qi, c):
        qrows = pl.ds(pl.multiple_of(qi * tile, tile), tile)
        qs = [q_ref[qrows, lanes(hh)] for hh in heads]
        _interleave(*[block_rounds(hh, qs[hh], qi, True) for hh in heads])

        def k_step(s, c2):
            _interleave(*[block_rounds(hh, qs[hh], qi - 1 - s, False) for hh in heads])
            return c2

        lax.fori_loop(0, qi, k_step, 0)
        for hh in heads:
            o_ref[qrows, lanes(hh)] = acc_s[hh].astype(o_ref.dtype)
        return c

    lax.fori_loop(0, seq // tile, q_block, 0)


def sb_attention(qkv, batch, seq, *, tile=512, sub=256, name):
    tile = min(tile, seq)
    sub = min(sub, tile)
    n_groups = SB_HEADS // SB_GROUP
    width = SB_GROUP * HEAD_DIM
    spec = lambda part: pl.BlockSpec((seq, width), lambda b, g: (b, part * n_groups + g))
    return pl.pallas_call(
        functools.partial(_sb_attn_kernel, seq=seq, tile=tile, sub=sub),
        grid=(batch, n_groups),
        in_specs=[spec(0), spec(1), spec(2)],
        out_specs=pl.BlockSpec((seq, width), lambda b, g: (b, g)),
        out_shape=jax.ShapeDtypeStruct((batch * seq, SB_WIDTH), BF16),
        scratch_shapes=[pltpu.VMEM((SB_GROUP, tile, HEAD_DIM), F32), pltpu.VMEM((SB_GROUP, tile, 1), F32)],
        compiler_params=_params(2), name=name,
    )(qkv, qkv, qkv)


def _gdn_kernel(alog_ref, dtb_ref, xq_ref, xk_ref, xv_ref, z_ref, cwq_ref, cwk_ref, cwv_ref,
                brow_ref, arow_ref, bcol_ref, acol_ref, ng_ref, o_ref,
                u_s, wq_s, kt_s, at_s, gcr_s, gcol_s, *, layer, seq):
    sup = min(GDN_SUPER, seq)
    per = sup // CHUNK
    n_sup = seq // sup
    group = pl.program_id(1)
    lanes = lambda hh: slice(hh * HEAD_DIM, (hh + 1) * HEAD_DIM)

    ri = lax.broadcasted_iota(jnp.int32, (sup, sup), 0)
    ci = lax.broadcasted_iota(jnp.int32, (sup, sup), 1)
    same = (ri // CHUNK) == (ci // CHUNK)
    in_lower = jnp.logical_and(same, ri >= ci)
    neg_inf_mask = jnp.where(in_lower, 0.0, -jnp.inf)
    neg_strict = jnp.where(jnp.logical_and(same, ri > ci), -1.0, 0.0)
    tri_lower = jnp.where(in_lower, 1.0, 0.0)
    tri_upper = jnp.where(jnp.logical_and(same, ri <= ci), 1.0, 0.0)
    block_ones = jnp.where(same, 1.0, 0.0)
    sup_lane = lax.broadcasted_iota(jnp.int32, (sup, n_sup), 1)

    def l2norm(y):
        return y * lax.rsqrt(jnp.sum(y * y, axis=-1, keepdims=True) + EPS)

    def conv_silu(x_ref, cw_ref, hh, sc):
        cw = cw_ref[:, lanes(hh)]
        xb = x_ref[pl.ds(_aligned(sc * sup, sup), sup), lanes(hh)]
        if isinstance(sc, int) and sc == 0:
            prev = jnp.zeros((8, HEAD_DIM), F32)
        else:
            prev = x_ref[pl.ds(pl.multiple_of(jnp.maximum(sc * sup - 8, 0), 8), 8), lanes(hh)]
            prev = jnp.where(sc > 0, prev, 0.0)
        xx = jnp.concatenate([prev, xb], axis=0)
        y = cw[SHORT_CONV - 1:SHORT_CONV, :] * xx[8:, :]
        for tap in range(SHORT_CONV - 1):
            shift = SHORT_CONV - 1 - tap
            y = y + cw[tap:tap + 1, :] * xx[8 - shift:8 - shift + sup, :]
        return _silu(y)

    hdot = functools.partial(jnp.dot, precision=lax.Precision.HIGHEST, preferred_element_type=F32)
    for hh in range(GDN_GROUP):
        head = group * GDN_GROUP + hh
        neg_a = -jnp.exp(jnp.full((1, 1), alog_ref[layer, head], F32))
        dtb = dtb_ref[layer, head]
        g_row = neg_a * _softplus(arow_ref[hh] + dtb)
        g_col = neg_a * _softplus(acol_ref[hh] + dtb)
        gcr_s[hh] = hdot(g_row, tri_upper)
        gcol_s[hh, 0] = _sigmoid(bcol_ref[hh])
        gcol_s[hh, 1] = hdot(tri_lower, g_col)
        gcol_s[hh, 2] = hdot(block_ones, g_col)

    heads = range(GDN_GROUP)
    n_sq = max(CHUNK.bit_length() - 2, 0)

    def prep_rounds(sc, slot):
        st = [dict() for _ in heads]
        for hh in heads:
            s = st[hh]
            sel = sup_lane == sc
            pick = lambda kind: jnp.sum(jnp.where(sel, gcol_s[hh, kind], 0.0), axis=1, keepdims=True)
            bcol, gcol, glast = pick(0), pick(1), pick(2)
            grow = gcr_s[hh, pl.ds(sc, 1), :]
            qc = l2norm(conv_silu(xq_ref, cwq_ref, hh, sc)) * (HEAD_DIM ** -0.5)
            kc = l2norm(conv_silu(xk_ref, cwk_ref, hh, sc))
            s["decay"] = jnp.exp((gcol - grow) + neg_inf_mask)
            kb = kc * bcol
            egc = jnp.exp(gcol)
            s["rhs"] = jnp.concatenate([conv_silu(xv_ref, cwv_ref, hh, sc) * bcol, kb * egc], axis=1)
            s["kb"], s["kc"], s["qc"] = kb.astype(BF16), kc.astype(BF16), qc.astype(BF16)
            qd = (qc * egc).astype(BF16)
            for p in range(per):
                blk = slice(p * CHUNK, (p + 1) * CHUNK)
                wq_s[slot, hh, p, CHUNK:, :] = qd[blk, :]
            kt_s[slot, hh] = (kc * jnp.exp(glast - gcol)).astype(BF16)
        yield
        for hh in heads:
            s = st[hh]
            s["y"] = _dot_nt(s["kb"], s["kc"]) * s["decay"] * neg_strict
            s["z"] = s["y"]
            attn = (_dot_nt(s["qc"], s["kc"]) * s["decay"]).astype(BF16)
            for p in range(per):
                blk = slice(p * CHUNK, (p + 1) * CHUNK)
                at_s[slot, hh, blk, :] = attn[blk, blk]
        yield
        for m in range(n_sq):
            for hh in heads:
                s = st[hh]
                yb = s["y"].astype(BF16)
                if m > 0:
                    s["z"] = s["z"] + s["y"] + _dot(s["z"].astype(BF16), yb)
                s["y"] = _dot(yb, yb)
            yield
        for hh in heads:
            s = st[hh]
            s["z"] = s["z"] + s["y"] + _dot(s["z"].astype(BF16), s["y"].astype(BF16))
        yield
        for hh in heads:
            s = st[hh]
            uw = s["rhs"] + _dot(s["z"].astype(BF16), s["rhs"].astype(BF16))
            u_s[slot, hh] = uw[:, :HEAD_DIM]
            w = uw[:, HEAD_DIM:].astype(BF16)
            for p in range(per):
                blk = slice(p * CHUNK, (p + 1) * CHUNK)
                wq_s[slot, hh, p, :CHUNK, :] = w[blk, :]
        yield

    ng = ng_ref[...]

    def scan_rounds(sc, slot, states):
        chunk_decay = [jnp.exp(gcr_s[hh, pl.ds(sc, 1), :]) for hh in heads]
        for p in range(per):
            blk = slice(p * CHUNK, (p + 1) * CHUNK)
            rows = pl.ds(_aligned(sc * sup + p * CHUNK, CHUNK), CHUNK)
            vnb, qs = [None] * GDN_GROUP, [None] * GDN_GROUP
            for hh in heads:
                both = _dot(wq_s[slot, hh, p], states[hh].astype(BF16))
                vnb[hh] = (u_s[slot, hh, blk, :] - both[:CHUNK, :]).astype(BF16)
                qs[hh] = both[CHUNK:, :]
            yield
            for hh in heads:
                o = qs[hh] + _dot(at_s[slot, hh, blk, :], vnb[hh])
                last = p * CHUNK + CHUNK - 1
                cd = chunk_decay[hh][:, last:last + 1]
                states[hh] = states[hh] * cd + _dot_tn(kt_s[slot, hh, blk, :], vnb[hh])
                o_ref[rows, lanes(hh)] = (_rmsnorm(o, ng) * _silu(z_ref[rows, lanes(hh)])).astype(o_ref.dtype)
            yield

    def step(i, states):
        states = list(states)
        slot = lax.rem(i, 2)
        _interleave(prep_rounds(i, slot), scan_rounds(i - 1, 1 - slot, states))
        return tuple(states)

    _interleave(prep_rounds(0, 0))
    states = tuple(jnp.zeros((HEAD_DIM, HEAD_DIM), F32) for _ in heads)
    states = list(lax.fori_loop(1, n_sup, step, states))
    _interleave(scan_rounds(n_sup - 1, (n_sup - 1) % 2, states))


def gated_deltanet(qkvz, ba_rows, ba_cols, conv_w, a_log, dt_bias, norm_g, layer, batch, seq, *, name):
    sup = min(GDN_SUPER, seq)
    n_sup = seq // sup
    n_groups = GDN_HEADS // GDN_GROUP
    width = GDN_GROUP * HEAD_DIM
    col = lambda part: pl.BlockSpec((seq, width), lambda b, g: (b, part * n_groups + g))
    cw = lambda part: pl.BlockSpec((None, SHORT_CONV, width), lambda b, g: (layer, 0, part * n_groups + g))
    rows = lambda part: pl.BlockSpec((None, GDN_GROUP, n_sup, sup), lambda b, g: (b, part * n_groups + g, 0, 0))
    cols = lambda part: pl.BlockSpec((None, GDN_GROUP, sup, n_sup), lambda b, g: (b, part * n_groups + g, 0, 0))
    smem = pl.BlockSpec(memory_space=pltpu.SMEM)
    per_head = lambda shape, dt: pltpu.VMEM((GDN_GROUP,) + shape, dt)
    slots = lambda shape, dt: pltpu.VMEM((2, GDN_GROUP) + shape, dt)
    return pl.pallas_call(
        functools.partial(_gdn_kernel, layer=layer, seq=seq),
        grid=(batch, n_groups),
        in_specs=[smem, smem, col(0), col(1), col(2), col(3), cw(0), cw(1), cw(2),
                  rows(0), rows(1), cols(0), cols(1),
                  pl.BlockSpec((None, 1, HEAD_DIM), lambda b, g: (layer, 0, 0))],
        out_specs=pl.BlockSpec((seq, width), lambda b, g: (b, g)),
        out_shape=jax.ShapeDtypeStruct((batch * seq, GDN_WIDTH), BF16),
        scratch_shapes=[
            slots((sup, HEAD_DIM), F32),
            slots((sup // CHUNK, 2 * CHUNK, HEAD_DIM), BF16),
            slots((sup, HEAD_DIM), BF16),
            slots((sup, CHUNK), BF16),
            per_head((n_sup, sup), F32),
            per_head((3, sup, n_sup), F32),
        ],
        compiler_params=_params(2), name=name,
    )(a_log, dt_bias, qkvz, qkvz, qkvz, qkvz, conv_w, conv_w, conv_w,
      ba_rows, ba_rows, ba_cols, ba_cols, norm_g)


def _proj2_residual_kernel(x_ref, a1_ref, a2_ref, w1_ref, w2_ref, o_ref):
    o_ref[...] = x_ref[...] + _dot(a1_ref[...], w1_ref[...]) + _dot(a2_ref[...], w2_ref[...])


def proj2_residual(x, a1, a2, w, layer, *, tm, tn, name):
    m, n = x.shape
    k1, k2 = a1.shape[1], a2.shape[1]
    assert k1 == k2
    tm = min(tm, m)
    return pl.pallas_call(
        _proj2_residual_kernel,
        grid=(m // tm, n // tn),
        in_specs=[
            pl.BlockSpec((tm, tn), lambda i, j: (i, j)),
            pl.BlockSpec((tm, k1), lambda i, j: (i, 0)),
            pl.BlockSpec((tm, k2), lambda i, j: (i, 0)),
            pl.BlockSpec((None, k1, tn), lambda i, j: (layer, 0, j)),
            pl.BlockSpec((None, k2, tn), lambda i, j: (layer, 1, j)),
        ],
        out_specs=pl.BlockSpec((tm, tn), lambda i, j: (i, j)),
        out_shape=jax.ShapeDtypeStruct((m, n), F32),
        compiler_params=_params(2), name=name,
    )(x, a1, a2, w, w)


def _xattn_kernel(x_ref, g_ref, wq_ref, k_ref, v_ref, wo_ref, gn_ref, o_ref, hn_ref, h_s, a_s, *, tm):
    _rmsnorm_into(h_s, 0, x_ref, g_ref[...], tm)
    q = _dot(h_s[...], wq_ref[...]).astype(BF16)
    scale = HEAD_DIM ** -0.5
    for hd in range(X_HEADS):
        cols = slice(hd * HEAD_DIM, (hd + 1) * HEAD_DIM)
        s = _dot_nt(q[:, cols], k_ref[:, cols]) * scale
        p = jnp.exp(s - jnp.max(s, axis=-1, keepdims=True))
        denom = jnp.sum(p, axis=-1, keepdims=True)
        a_s[:, cols] = (_dot(p.astype(BF16), v_ref[:, cols]) / denom).astype(BF16)
    o_ref[...] = x_ref[...] + _dot(a_s[...], wo_ref[...])
    _rmsnorm_into(hn_ref, 0, o_ref, gn_ref[...], tm)


def cross_attention_block(x, g, wq, kv, wo, g_next, layer, seq, n_mem, *, tm, name):
    m, d = x.shape
    tm = min(tm, seq)
    per_seq = seq // tm
    gain = pl.BlockSpec((None, 1, d), lambda i: (layer, 0, 0))
    rows = pl.BlockSpec((tm, d), lambda i: (i, 0))
    return pl.pallas_call(
        functools.partial(_xattn_kernel, tm=tm),
        grid=(m // tm,),
        in_specs=[
            rows, gain,
            pl.BlockSpec((None, d, X_WIDTH), lambda i: (layer, 0, 0)),
            pl.BlockSpec((n_mem, X_WIDTH), lambda i: (i // per_seq, 0)),
            pl.BlockSpec((n_mem, X_WIDTH), lambda i: (i // per_seq, 1)),
            pl.BlockSpec((None, X_WIDTH, d), lambda i: (layer, 0, 0)),
            gain,
        ],
        out_specs=[rows, rows],
        out_shape=[jax.ShapeDtypeStruct((m, d), F32), jax.ShapeDtypeStruct((m, d), BF16)],
        scratch_shapes=[pltpu.VMEM((tm, d), BF16), pltpu.VMEM((tm, X_WIDTH), BF16)],
        compiler_params=_params(1), name=name,
    )(x, g, wq, kv, kv, wo, g_next)


def _ffn_up_kernel(h_ref, halo_ref, wg_ref, wu_ref, cg_ref, cu_ref, bg_ref, bu_ref,
                   o_ref, h_s, *, tm, tiles_per_seq):
    i = pl.program_id(0)

    @pl.when(pl.program_id(1) == 0)
    def _():
        keep = jnp.where(i % tiles_per_seq == 0, 0.0, 1.0).astype(h_s.dtype)
        h_s[:HALO_ROWS, :] = halo_ref[...] * keep
        step = min(COPY_ROWS, tm)

        def body(r, carry):
            src = pl.ds(pl.multiple_of(r * step, step), step)
            dst = pl.ds(pl.multiple_of(HALO_ROWS + r * step, HALO_ROWS), step)
            h_s[dst, :] = h_ref[src, :]
            return carry

        lax.fori_loop(0, tm // step, body, 0)

    def conv(y, cw_ref, b_ref):
        cw = cw_ref[...]
        out = cw[FFN_CONV - 1:FFN_CONV, :] * y[HALO_ROWS:, :]
        for tap in range(FFN_CONV - 1):
            shift = FFN_CONV - 1 - tap
            out = out + cw[tap:tap + 1, :] * y[HALO_ROWS - shift:HALO_ROWS - shift + tm, :]
        return out + b_ref[...]

    h = h_s[...]
    gate = conv(_dot(h, wg_ref[...]), cg_ref, bg_ref)
    up = conv(_dot(h, wu_ref[...]), cu_ref, bu_ref)
    o_ref[...] = (_silu(gate) * up).astype(o_ref.dtype)


def ffn_up(h, w_up, conv_w, conv_b, layer, seq, *, tm, tn, name):
    m, d = h.shape
    d_ff = w_up.shape[-1] // 2
    tm = min(tm, seq)
    assert d_ff % tn == 0 and seq % tm == 0 and tm % HALO_ROWS == 0
    nj = d_ff // tn
    halo_blocks = tm // HALO_ROWS
    return pl.pallas_call(
        functools.partial(_ffn_up_kernel, tm=tm, tiles_per_seq=seq // tm),
        grid=(m // tm, nj),
        in_specs=[
            pl.BlockSpec((tm, d), lambda i, j: (i, 0)),
            pl.BlockSpec((HALO_ROWS, d), lambda i, j: (jnp.maximum(i * halo_blocks - 1, 0), 0)),
            pl.BlockSpec((None, d, tn), lambda i, j: (layer, 0, j)),
            pl.BlockSpec((None, d, tn), lambda i, j: (layer, 0, nj + j)),
            pl.BlockSpec((None, FFN_CONV, tn), lambda i, j: (layer, 0, j)),
            pl.BlockSpec((None, FFN_CONV, tn), lambda i, j: (layer, 0, nj + j)),
            pl.BlockSpec((None, 1, tn), lambda i, j: (layer, 0, j)),
            pl.BlockSpec((None, 1, tn), lambda i, j: (layer, 0, nj + j)),
        ],
        out_specs=pl.BlockSpec((tm, tn), lambda i, j: (i, j)),
        out_shape=jax.ShapeDtypeStruct((m, d_ff), BF16),
        scratch_shapes=[pltpu.VMEM((tm + HALO_ROWS, d), BF16)],
        compiler_params=_params(2), name=name,
    )(h, h, w_up, w_up, conv_w, conv_w, conv_b, conv_b)


def _matmul_residual_kernel(x_ref, a_ref, w_ref, o_ref):
    o_ref[...] = x_ref[...] + _dot(a_ref[...], w_ref[...])


def matmul_residual(x, a, w, layer, *, tm, tn, name):
    m, n = x.shape
    k = a.shape[1]
    tm = min(tm, m)
    return pl.pallas_call(
        _matmul_residual_kernel,
        grid=(m // tm, n // tn),
        in_specs=[
            pl.BlockSpec((tm, tn), lambda i, j: (i, j)),
            pl.BlockSpec((tm, k), lambda i, j: (i, 0)),
            pl.BlockSpec((None, k, tn), lambda i, j: (layer, 0, j)),
        ],
        out_specs=pl.BlockSpec((tm, tn), lambda i, j: (i, j)),
        out_shape=jax.ShapeDtypeStruct((m, n), F32),
        compiler_params=_params(2), name=name,
    )(x, a, w)


def _final_norm_kernel(x_ref, g_ref, o_ref, *, tm):
    _rmsnorm_into(o_ref, 0, x_ref, g_ref[...], tm)


def final_norm(x, g, *, tm, name):
    m, d = x.shape
    tm = min(tm, m)
    return pl.pallas_call(
        functools.partial(_final_norm_kernel, tm=tm),
        grid=(m // tm,),
        in_specs=[pl.BlockSpec((tm, d), lambda i: (i, 0)), pl.BlockSpec((1, d), lambda i: (0, 0))],
        out_specs=pl.BlockSpec((tm, d), lambda i: (i, 0)),
        out_shape=jax.ShapeDtypeStruct((m, d), F32),
        compiler_params=_params(1), name=name,
    )(x, g)


def kernel(x, mem, mix_norm, w_in, gdn_conv, gdn_a_log, gdn_dt_bias, gdn_norm, w_out, xattn_norm,
           mem_norm, w_xq, w_xkv, w_xo, ffn_norm, w_up, ffn_conv, ffn_conv_bias, w_down, final_norm_g):
    batch, seq, d_model = x.shape
    n_mem = mem.shape[1]
    depth = w_in.shape[0]
    sup = min(GDN_SUPER, seq)
    n_gate = 2 * GDN_HEADS

    w_out_b = w_out.astype(BF16)
    w_down_b = w_down.astype(BF16)
    w_in_t = jnp.swapaxes(w_in, 1, 2)
    row3 = lambda a: a.reshape(depth, 1, a.shape[-1])
    mix_g, xat_g, mem_g, ffn_g = row3(mix_norm), row3(xattn_norm), row3(mem_norm), row3(ffn_norm)
    gdn_g = row3(gdn_norm)
    conv_b = row3(ffn_conv_bias)

    xf = x.reshape(batch * seq, d_model)
    memf = mem.reshape(batch * n_mem, d_model)

    for l in range(depth):
        sb_qkv, gdn_qkvz, gates = in_projection(xf, mix_g, w_in_t, l, 3 * SB_WIDTH, 4 * GDN_WIDTH,
                                                tm=2048, tn=512, name=f"inproj_{l}")
        sb_out = sb_attention(sb_qkv, batch, seq, name=f"sb_attn_{l}")
        ba = gates.reshape(batch, seq, n_gate).transpose(0, 2, 1)
        ba_rows = ba.reshape(batch, n_gate, seq // sup, sup)
        ba_cols = ba_rows.transpose(0, 1, 3, 2)
        gdn_out = gated_deltanet(gdn_qkvz, ba_rows, ba_cols, gdn_conv, gdn_a_log, gdn_dt_bias, gdn_g,
                                 l, batch, seq, name=f"gdn_{l}")
        xf = proj2_residual(xf, sb_out, gdn_out, w_out_b, l, tm=1024, tn=1024, name=f"outproj_{l}")

        kv = norm_matmul(memf, mem_g, w_xkv, l, BF16, tm=512, tn=512, name=f"mem_kv_{l}")
        xf, h_ffn = cross_attention_block(xf, xat_g, w_xq, kv, w_xo, ffn_g, l, seq, n_mem, tm=512,
                                          name=f"xattn_{l}")

        act = ffn_up(h_ffn, w_up, ffn_conv, conv_b, l, seq, tm=1024, tn=512, name=f"ffn_up_{l}")
        xf = matmul_residual(xf, act, w_down_b, l, tm=1024, tn=512, name=f"ffn_down_{l}")

    out = final_norm(xf, final_norm_g.reshape(1, d_model), tm=512, name="final_norm")
    return out.reshape(batch, seq, d_model)
```

```python
import functools
import math

import jax
import jax.numpy as jnp
from jax import lax
from jax.experimental import pallas as pl
from jax.experimental.pallas import tpu as pltpu

F32 = jnp.float32
BF16 = jnp.bfloat16

HEAD_DIM = 128
SB_HEADS = 8
SB_GROUP = 4
GDN_HEADS = 8
SB_WIDTH = SB_HEADS * HEAD_DIM
GDN_WIDTH = GDN_HEADS * HEAD_DIM
SHORT_CONV = 4
CHUNK = 64
GDN_SUPER = 256
GDN_GROUP = 4
X_HEADS = 4
X_WIDTH = X_HEADS * HEAD_DIM
FFN_CONV = 3
EPS = 1e-6

V7X_VMEM_BYTES = 64 * 1024 * 1024
VMEM_LIMIT_BYTES = V7X_VMEM_BYTES - 8 * 1024 * 1024
TILES = {
    "in_proj": dict(tm=2048, tn=512),
    "out_proj": dict(tm=1024, tn=1024),
    "mem_kv": dict(tm=512, tn=512),
    "xattn": dict(tm=512),
    "ffn_up": dict(tm=1024, tn=512),
    "ffn_down": dict(tm=1024, tn=512),
    "final_norm": dict(tm=512),
}
DOT_ROWS = 1024
NORM_ROWS = 64
HALO_ROWS = 16
COPY_ROWS = 128


def _params(n_axes):
    return pltpu.CompilerParams(
        dimension_semantics=("arbitrary",) * n_axes, vmem_limit_bytes=VMEM_LIMIT_BYTES)


def _rmsnorm(x, g):
    ms = jnp.mean(x * x, axis=-1, keepdims=True)
    return x * lax.rsqrt(ms + EPS) * g


def _rmsnorm_into(dst_ref, dst_row0, x_ref, g, n_rows, scale=None):
    step = min(NORM_ROWS, n_rows)
    dst_align = math.gcd(step, dst_row0) if dst_row0 else step

    def body(r, carry):
        src = pl.ds(pl.multiple_of(r * step, step), step)
        dst = pl.ds(pl.multiple_of(dst_row0 + r * step, dst_align), step)
        y = _rmsnorm(x_ref[src, :], g)
        if scale is not None:
            y = y * scale
        dst_ref[dst, :] = y.astype(dst_ref.dtype)
        return carry

    lax.fori_loop(0, n_rows // step, body, 0)


def _aligned(start, multiple):
    return start if isinstance(start, int) else pl.multiple_of(start, multiple)


def _interleave(*gens):
    live = list(gens)
    while live:
        for g in list(live):
            if next(g, live) is live:
                live.remove(g)


def _sigmoid(x):
    return 1.0 / (1.0 + jnp.exp(-x))


def _silu(x):
    return x * _sigmoid(x)


def _softplus(x):
    return jnp.maximum(x, 0.0) + jnp.log1p(jnp.exp(-jnp.abs(x)))


def _dot(a, b):
    return lax.dot_general(a, b, (((1,), (0,)), ((), ())), preferred_element_type=F32)


def _dot_nt(a, b):
    return lax.dot_general(a, b, (((1,), (1,)), ((), ())), preferred_element_type=F32)


def _dot_tn(a, b):
    return lax.dot_general(a, b, (((0,), (0,)), ((), ())), preferred_element_type=F32)


def _norm_matmul_kernel(x_ref, g_ref, w_ref, o_ref, h_ref, *, tm):
    @pl.when(pl.program_id(1) == 0)
    def _():
        _rmsnorm_into(h_ref, 0, x_ref, g_ref[...], tm)

    o_ref[...] = _dot(h_ref[...], w_ref[...]).astype(o_ref.dtype)


def norm_matmul(x, g, w, layer, out_dtype, *, tm, tn, name):
    m, k = x.shape
    n = w.shape[-1]
    tm = min(tm, m)
    assert m % tm == 0 and n % tn == 0
    return pl.pallas_call(
        functools.partial(_norm_matmul_kernel, tm=tm),
        grid=(m // tm, n // tn),
        in_specs=[
            pl.BlockSpec((tm, k), lambda i, j: (i, 0)),
            pl.BlockSpec((None, 1, k), lambda i, j: (layer, 0, 0)),
            pl.BlockSpec((None, k, tn), lambda i, j: (layer, 0, j)),
        ],
        out_specs=pl.BlockSpec((tm, tn), lambda i, j: (i, j)),
        out_shape=jax.ShapeDtypeStruct((m, n), out_dtype),
        scratch_shapes=[pltpu.VMEM((tm, k), BF16)],
        compiler_params=_params(2), name=name,
    )(x, g, w)


def _in_proj_kernel(x_ref, g_ref, wt_ref, wst_ref, osb_ref, ogdn_ref, ogate_ref, h_ref, *, tm, n_sb):
    j = pl.program_id(1)

    @pl.when(j == 0)
    def _():
        _rmsnorm_into(h_ref, 0, x_ref, g_ref[...], tm)
        ogate_ref[...] = _dot_nt(h_ref[...], wst_ref[...])

    def project(o_ref):
        for r0 in range(0, tm, DOT_ROWS):
            rows = slice(r0, min(r0 + DOT_ROWS, tm))
            o_ref[rows, :] = _dot_nt(h_ref[rows, :], wt_ref[...]).astype(o_ref.dtype)

    @pl.when(j < n_sb)
    def _():
        project(osb_ref)

    @pl.when(j >= n_sb)
    def _():
        project(ogdn_ref)


def in_projection(x, g, w_t, layer, n_sb_cols, n_gdn_cols, *, tm, tn, name):
    m, k = x.shape
    tm = min(tm, m)
    n_main = n_sb_cols + n_gdn_cols
    n_gate = w_t.shape[1] - n_main
    assert m % tm == 0 and n_sb_cols % tn == 0 and n_gdn_cols % tn == 0
    assert n_gate > 0 and n_main % n_gate == 0 and n_gate % 8 == 0
    n_sb, n_gdn = n_sb_cols // tn, n_gdn_cols // tn
    return pl.pallas_call(
        functools.partial(_in_proj_kernel, tm=tm, n_sb=n_sb),
        grid=(m // tm, n_sb + n_gdn),
        in_specs=[
            pl.BlockSpec((tm, k), lambda i, j: (i, 0), pipeline_mode=pl.Buffered(1)),
            pl.BlockSpec((None, 1, k), lambda i, j: (layer, 0, 0)),
            pl.BlockSpec((None, tn, k), lambda i, j: (layer, j, 0)),
            pl.BlockSpec((None, n_gate, k), lambda i, j: (layer, n_main // n_gate, 0)),
        ],
        out_specs=[
            pl.BlockSpec((tm, tn), lambda i, j: (i, jnp.minimum(j, n_sb - 1))),
            pl.BlockSpec((tm, tn), lambda i, j: (i, jnp.maximum(j - n_sb, 0))),
            pl.BlockSpec((tm, n_gate), lambda i, j: (i, 0)),
        ],
        out_shape=[jax.ShapeDtypeStruct((m, n_sb_cols), BF16),
                   jax.ShapeDtypeStruct((m, n_gdn_cols), F32),
                   jax.ShapeDtypeStruct((m, n_gate), F32)],
        scratch_shapes=[pltpu.VMEM((tm, k), BF16)],
        compiler_params=_params(2), name=name,
    )(x, g, w_t, w_t)


def _sb_attn_kernel(q_ref, k_ref, v_ref, o_ref, acc_s, carry_s, *, seq, tile, sub):
    scale = HEAD_DIM ** -0.5
    n_sub = tile // sub
    r = lax.broadcasted_iota(jnp.int32, (sub, sub), 0)
    c = lax.broadcasted_iota(jnp.int32, (sub, sub), 1)
    suffix = jnp.where(r > c, 1.0, 0.0).astype(BF16)
    row = lax.broadcasted_iota(jnp.int32, (tile, tile), 0)
    col = lax.broadcasted_iota(jnp.int32, (tile, tile), 1)
    causal = col < row
    sign_bit = jnp.uint32(0x80000000)

    heads = range(SB_GROUP)
    lanes = lambda hh: slice(hh * HEAD_DIM, (hh + 1) * HEAD_DIM)

    def block_rounds(hh, q, kb, first):
        rows = pl.ds(pl.multiple_of(kb * tile, tile), tile)
        z = _dot_nt(q, k_ref[rows, lanes(hh)]) * scale
        yield
        neg_abs = lax.bitcast_convert_type(lax.bitcast_convert_type(z, jnp.uint32) | sign_bit, F32)
        log_beta = jnp.minimum(z, 0.0) - jnp.log(1.0 + jnp.exp(neg_abs))
        log_stay = log_beta - z
        if first:
            log_stay = jnp.where(causal, log_stay, 0.0)
        laters = [None] * n_sub
        run = None if first else carry_s[hh]
        for s in reversed(range(n_sub)):
            ls = log_stay[:, s * sub:(s + 1) * sub]
            cum = _dot(ls.astype(BF16), suffix)
            laters[s] = cum if run is None else cum + run
            tot = jnp.sum(ls, axis=-1, keepdims=True)
            run = tot if run is None else run + tot
        carry_s[hh] = run
        yield
        w = jnp.exp(log_beta + jnp.concatenate(laters, axis=1))
        if first:
            w = jnp.where(causal, w, 0.0)
        pv = _dot(w.astype(BF16), v_ref[rows, lanes(hh)])
        if first:
            acc_s[hh] = pv
        else:
            acc_s[hh] += pv
        yield

    def q_block(qi, c):
        qrows = pl.ds(pl.multiple_of(qi * tile, tile), tile)
        qs = [q_ref[qrows, lanes(hh)] for hh in heads]
        _interleave(*[block_rounds(hh, qs[hh], qi, True) for hh in heads])

        def k_step(s, c2):
            _interleave(*[block_rounds(hh, qs[hh], qi - 1 - s, False) for hh in heads])
            return c2

        lax.fori_loop(0, qi, k_step, 0)
        for hh in heads:
            o_ref[qrows, lanes(hh)] = acc_s[hh].astype(o_ref.dtype)
        return c

    lax.fori_loop(0, seq // tile, q_block, 0)


def sb_attention(qkv, batch, seq, *, tile=512, sub=256, name):
    tile = min(tile, seq)
    sub = min(sub, tile)
    n_groups = SB_HEADS // SB_GROUP
    width = SB_GROUP * HEAD_DIM
    spec = lambda part: pl.BlockSpec((seq, width), lambda b, g: (b, part * n_groups + g))
    return pl.pallas_call(
        functools.partial(_sb_attn_kernel, seq=seq, tile=tile, sub=sub),
        grid=(batch, n_groups),
        in_specs=[spec(0), spec(1), spec(2)],
        out_specs=pl.BlockSpec((seq, width), lambda b, g: (b, g)),
        out_shape=jax.ShapeDtypeStruct((batch * seq, SB_WIDTH), BF16),
        scratch_shapes=[pltpu.VMEM((SB_GROUP, tile, HEAD_DIM), F32), pltpu.VMEM((SB_GROUP, tile, 1), F32)],
        compiler_params=_params(2), name=name,
    )(qkv, qkv, qkv)


def _gdn_kernel(alog_ref, dtb_ref, xq_ref, xk_ref, xv_ref, z_ref, cwq_ref, cwk_ref, cwv_ref,
                brow_ref, arow_ref, bcol_ref, acol_ref, ng_ref, o_ref,
                u_s, wq_s, kt_s, at_s, gcr_s, gcol_s, *, layer, seq):
    sup = min(GDN_SUPER, seq)
    per = sup // CHUNK
    n_sup = seq // sup
    group = pl.program_id(1)
    lanes = lambda hh: slice(hh * HEAD_DIM, (hh + 1) * HEAD_DIM)

    ri = lax.broadcasted_iota(jnp.int32, (sup, sup), 0)
    ci = lax.broadcasted_iota(jnp.int32, (sup, sup), 1)
    same = (ri // CHUNK) == (ci // CHUNK)
    in_lower = jnp.logical_and(same, ri >= ci)
    neg_inf_mask = jnp.where(in_lower, 0.0, -jnp.inf)
    neg_strict = jnp.where(jnp.logical_and(same, ri > ci), -1.0, 0.0)
    tri_lower = jnp.where(in_lower, 1.0, 0.0)
    tri_upper = jnp.where(jnp.logical_and(same, ri <= ci), 1.0, 0.0)
    block_ones = jnp.where(same, 1.0, 0.0)
    sup_lane = lax.broadcasted_iota(jnp.int32, (sup, n_sup), 1)

    def l2norm(y):
        return y * lax.rsqrt(jnp.sum(y * y, axis=-1, keepdims=True) + EPS)

    def conv_silu(x_ref, cw_ref, hh, sc):
        cw = cw_ref[:, lanes(hh)]
        xb = x_ref[pl.ds(_aligned(sc * sup, sup), sup), lanes(hh)]
        if isinstance(sc, int) and sc == 0:
            prev = jnp.zeros((8, HEAD_DIM), F32)
        else:
            prev = x_ref[pl.ds(pl.multiple_of(jnp.maximum(sc * sup - 8, 0), 8), 8), lanes(hh)]
            prev = jnp.where(sc > 0, prev, 0.0)
        xx = jnp.concatenate([prev, xb], axis=0)
        y = cw[SHORT_CONV - 1:SHORT_CONV, :] * xx[8:, :]
        for tap in range(SHORT_CONV - 1):
            shift = SHORT_CONV - 1 - tap
            y = y + cw[tap:tap + 1, :] * xx[8 - shift:8 - shift + sup, :]
        return _silu(y)

    hdot = functools.partial(jnp.dot, precision=lax.Precision.HIGHEST, preferred_element_type=F32)
    for hh in range(GDN_GROUP):
        head = group * GDN_GROUP + hh
        neg_a = -jnp.exp(jnp.full((1, 1), alog_ref[layer, head], F32))
        dtb = dtb_ref[layer, head]
        g_row = neg_a * _softplus(arow_ref[hh] + dtb)
        g_col = neg_a * _softplus(acol_ref[hh] + dtb)
        gcr_s[hh] = hdot(g_row, tri_upper)
        gcol_s[hh, 0] = _sigmoid(bcol_ref[hh])
        gcol_s[hh, 1] = hdot(tri_lower, g_col)
        gcol_s[hh, 2] = hdot(block_ones, g_col)

    heads = range(GDN_GROUP)
    n_sq = max(CHUNK.bit_length() - 2, 0)

    def prep_rounds(sc, slot):
        st = [dict() for _ in heads]
        for hh in heads:
            s = st[hh]
            sel = sup_lane == sc
            pick = lambda kind: jnp.sum(jnp.where(sel, gcol_s[hh, kind], 0.0), axis=1, keepdims=True)
            bcol, gcol, glast = pick(0), pick(1), pick(2)
            grow = gcr_s[hh, pl.ds(sc, 1), :]
            qc = l2norm(conv_silu(xq_ref, cwq_ref, hh, sc)) * (HEAD_DIM ** -0.5)
            kc = l2norm(conv_silu(xk_ref, cwk_ref, hh, sc))
            s["decay"] = jnp.exp((gcol - grow) + neg_inf_mask)
            kb = kc * bcol
            egc = jnp.exp(gcol)
            s["rhs"] = jnp.concatenate([conv_silu(xv_ref, cwv_ref, hh, sc) * bcol, kb * egc], axis=1)
            s["kb"], s["kc"], s["qc"] = kb.astype(BF16), kc.astype(BF16), qc.astype(BF16)
            qd = (qc * egc).astype(BF16)
            for p in range(per):
                blk = slice(p * CHUNK, (p + 1) * CHUNK)
                wq_s[slot, hh, p, CHUNK:, :] = qd[blk, :]
            kt_s[slot, hh] = (kc * jnp.exp(glast - gcol)).astype(BF16)
        yield
        for hh in heads:
            s = st[hh]
            s["y"] = _dot_nt(s["kb"], s["kc"]) * s["decay"] * neg_strict
            s["z"] = s["y"]
            attn = (_dot_nt(s["qc"], s["kc"]) * s["decay"]).astype(BF16)
            for p in range(per):
                blk = slice(p * CHUNK, (p + 1) * CHUNK)
                at_s[slot, hh, blk, :] = attn[blk, blk]
        yield
        for m in range(n_sq):
            for hh in heads:
                s = st[hh]
                yb = s["y"].astype(BF16)
                if m > 0:
                    s["z"] = s["z"] + s["y"] + _dot(s["z"].astype(BF16), yb)
                s["y"] = _dot(yb, yb)
            yield
        for hh in heads:
            s = st[hh]
            s["z"] = s["z"] + s["y"] + _dot(s["z"].astype(BF16), s["y"].astype(BF16))
        yield
        for hh in heads:
            s = st[hh]
            uw = s["rhs"] + _dot(s["z"].astype(BF16), s["rhs"].astype(BF16))
            u_s[slot, hh] = uw[:, :HEAD_DIM]
            w = uw[:, HEAD_DIM:].astype(BF16)
            for p in range(per):
                blk = slice(p * CHUNK, (p + 1) * CHUNK)
                wq_s[slot, hh, p, :CHUNK, :] = w[blk, :]
        yield

    ng = ng_ref[...]

    def scan_rounds(sc, slot, states):
        chunk_decay = [jnp.exp(gcr_s[hh, pl.ds(sc, 1), :]) for hh in heads]
        for p in range(per):
            blk = slice(p * CHUNK, (p + 1) * CHUNK)
            rows = pl.ds(_aligned(sc * sup + p * CHUNK, CHUNK), CHUNK)
            vnb, qs = [None] * GDN_GROUP, [None] * GDN_GROUP
            for hh in heads:
                both = _dot(wq_s[slot, hh, p], states[hh].astype(BF16))
                vnb[hh] = (u_s[slot, hh, blk, :] - both[:CHUNK, :]).astype(BF16)
                qs[hh] = both[CHUNK:, :]
            yield
            for hh in heads:
                o = qs[hh] + _dot(at_s[slot, hh, blk, :], vnb[hh])
                last = p * CHUNK + CHUNK - 1
                cd = chunk_decay[hh][:, last:last + 1]
                states[hh] = states[hh] * cd + _dot_tn(kt_s[slot, hh, blk, :], vnb[hh])
                o_ref[rows, lanes(hh)] = (_rmsnorm(o, ng) * _silu(z_ref[rows, lanes(hh)])).astype(o_ref.dtype)
            yield

    def step(i, states):
        states = list(states)
        slot = lax.rem(i, 2)
        _interleave(prep_rounds(i, slot), scan_rounds(i - 1, 1 - slot, states))
        return tuple(states)

    _interleave(prep_rounds(0, 0))
    states = tuple(jnp.zeros((HEAD_DIM, HEAD_DIM), F32) for _ in heads)
    states = list(lax.fori_loop(1, n_sup, step, states))
    _interleave(scan_rounds(n_sup - 1, (n_sup - 1) % 2, states))


def gated_deltanet(qkvz, ba_rows, ba_cols, conv_w, a_log, dt_bias, norm_g, layer, batch, seq, *, name):
    sup = min(GDN_SUPER, seq)
    n_sup = seq // sup
    n_groups = GDN_HEADS // GDN_GROUP
    width = GDN_GROUP * HEAD_DIM
    col = lambda part: pl.BlockSpec((seq, width), lambda b, g: (b, part * n_groups + g))
    cw = lambda part: pl.BlockSpec((None, SHORT_CONV, width), lambda b, g: (layer, 0, part * n_groups + g))
    rows = lambda part: pl.BlockSpec((None, GDN_GROUP, n_sup, sup), lambda b, g: (b, part * n_groups + g, 0, 0))
    cols = lambda part: pl.BlockSpec((None, GDN_GROUP, sup, n_sup), lambda b, g: (b, part * n_groups + g, 0, 0))
    smem = pl.BlockSpec(memory_space=pltpu.SMEM)
    per_head = lambda shape, dt: pltpu.VMEM((GDN_GROUP,) + shape, dt)
    slots = lambda shape, dt: pltpu.VMEM((2, GDN_GROUP) + shape, dt)
    return pl.pallas_call(
        functools.partial(_gdn_kernel, layer=layer, seq=seq),
        grid=(batch, n_groups),
        in_specs=[smem, smem, col(0), col(1), col(2), col(3), cw(0), cw(1), cw(2),
                  rows(0), rows(1), cols(0), cols(1),
                  pl.BlockSpec((None, 1, HEAD_DIM), lambda b, g: (layer, 0, 0))],
        out_specs=pl.BlockSpec((seq, width), lambda b, g: (b, g)),
        out_shape=jax.ShapeDtypeStruct((batch * seq, GDN_WIDTH), BF16),
        scratch_shapes=[
            slots((sup, HEAD_DIM), F32),
            slots((sup // CHUNK, 2 * CHUNK, HEAD_DIM), BF16),
            slots((sup, HEAD_DIM), BF16),
            slots((sup, CHUNK), BF16),
            per_head((n_sup, sup), F32),
            per_head((3, sup, n_sup), F32),
        ],
        compiler_params=_params(2), name=name,
    )(a_log, dt_bias, qkvz, qkvz, qkvz, qkvz, conv_w, conv_w, conv_w,
      ba_rows, ba_rows, ba_cols, ba_cols, norm_g)


def _proj2_residual_kernel(x_ref, a1_ref, a2_ref, w1_ref, w2_ref, o_ref):
    o_ref[...] = x_ref[...] + _dot(a1_ref[...], w1_ref[...]) + _dot(a2_ref[...], w2_ref[...])


def proj2_residual(x, a1, a2, w, layer, *, tm, tn, name):
    m, n = x.shape
    k1, k2 = a1.shape[1], a2.shape[1]
    assert k1 == k2
    tm = min(tm, m)
    return pl.pallas_call(
        _proj2_residual_kernel,
        grid=(m // tm, n // tn),
        in_specs=[
            pl.BlockSpec((tm, tn), lambda i, j: (i, j)),
            pl.BlockSpec((tm, k1), lambda i, j: (i, 0)),
            pl.BlockSpec((tm, k2), lambda i, j: (i, 0)),
            pl.BlockSpec((None, k1, tn), lambda i, j: (layer, 0, j)),
            pl.BlockSpec((None, k2, tn), lambda i, j: (layer, 1, j)),
        ],
        out_specs=pl.BlockSpec((tm, tn), lambda i, j: (i, j)),
        out_shape=jax.ShapeDtypeStruct((m, n), F32),
        compiler_params=_params(2), name=name,
    )(x, a1, a2, w, w)


def _xattn_kernel(x_ref, g_ref, wq_ref, k_ref, v_ref, wo_ref, gn_ref, o_ref, hn_ref, h_s, a_s, *, tm):
    _rmsnorm_into(h_s, 0, x_ref, g_ref[...], tm)
    q = _dot(h_s[...], wq_ref[...]).astype(BF16)
    scale = HEAD_DIM ** -0.5
    cols = [slice(hd * HEAD_DIM, (hd + 1) * HEAD_DIM) for hd in range(X_HEADS)]
    logits = [_dot_nt(q[:, c], k_ref[:, c]) * scale for c in cols]
    probs = [jnp.exp(s - jnp.max(s, axis=-1, keepdims=True)) for s in logits]
    outs = [_dot(p.astype(BF16), v_ref[:, c]) for p, c in zip(probs, cols)]
    for p, o, c in zip(probs, outs, cols):
        a_s[:, c] = (o / jnp.sum(p, axis=-1, keepdims=True)).astype(BF16)
    o_ref[...] = x_ref[...] + _dot(a_s[...], wo_ref[...])
    _rmsnorm_into(hn_ref, 0, o_ref, gn_ref[...], tm)


def cross_attention_block(x, g, wq, kv, wo, g_next, layer, seq, n_mem, *, tm, name):
    m, d = x.shape
    tm = min(tm, seq)
    per_seq = seq // tm
    gain = pl.BlockSpec((None, 1, d), lambda i: (layer, 0, 0))
    rows = pl.BlockSpec((tm, d), lambda i: (i, 0))
    return pl.pallas_call(
        functools.partial(_xattn_kernel, tm=tm),
        grid=(m // tm,),
        in_specs=[
            rows, gain,
            pl.BlockSpec((None, d, X_WIDTH), lambda i: (layer, 0, 0)),
            pl.BlockSpec((n_mem, X_WIDTH), lambda i: (i // per_seq, 0)),
            pl.BlockSpec((n_mem, X_WIDTH), lambda i: (i // per_seq, 1)),
            pl.BlockSpec((None, X_WIDTH, d), lambda i: (layer, 0, 0)),
            gain,
        ],
        out_specs=[rows, rows],
        out_shape=[jax.ShapeDtypeStruct((m, d), F32), jax.ShapeDtypeStruct((m, d), BF16)],
        scratch_shapes=[pltpu.VMEM((tm, d), BF16), pltpu.VMEM((tm, X_WIDTH), BF16)],
        compiler_params=_params(1), name=name,
    )(x, g, wq, kv, kv, wo, g_next)


def _ffn_up_kernel(h_ref, halo_ref, wg_ref, wu_ref, cg_ref, cu_ref, bg_ref, bu_ref,
                   o_ref, h_s, *, tm, tiles_per_seq):
    i = pl.program_id(0)

    @pl.when(pl.program_id(1) == 0)
    def _():
        keep = jnp.where(i % tiles_per_seq == 0, 0.0, 1.0).astype(h_s.dtype)
        h_s[:HALO_ROWS, :] = halo_ref[...] * keep
        step = min(COPY_ROWS, tm)

        def body(r, carry):
            src = pl.ds(pl.multiple_of(r * step, step), step)
            dst = pl.ds(pl.multiple_of(HALO_ROWS + r * step, HALO_ROWS), step)
            h_s[dst, :] = h_ref[src, :]
            return carry

        lax.fori_loop(0, tm // step, body, 0)

    def conv(y, cw_ref, b_ref):
        cw = cw_ref[...]
        out = cw[FFN_CONV - 1:FFN_CONV, :] * y[HALO_ROWS:, :]
        for tap in range(FFN_CONV - 1):
            shift = FFN_CONV - 1 - tap
            out = out + cw[tap:tap + 1, :] * y[HALO_ROWS - shift:HALO_ROWS - shift + tm, :]
        return out + b_ref[...]

    h = h_s[...]
    gate = conv(_dot(h, wg_ref[...]), cg_ref, bg_ref)
    up = conv(_dot(h, wu_ref[...]), cu_ref, bu_ref)
    o_ref[...] = (_silu(gate) * up).astype(o_ref.dtype)


def ffn_up(h, w_up, conv_w, conv_b, layer, seq, *, tm, tn, name):
    m, d = h.shape
    d_ff = w_up.shape[-1] // 2
    tm = min(tm, seq)
    assert d_ff % tn == 0 and seq % tm == 0 and tm % HALO_ROWS == 0
    nj = d_ff // tn
    halo_blocks = tm // HALO_ROWS
    return pl.pallas_call(
        functools.partial(_ffn_up_kernel, tm=tm, tiles_per_seq=seq // tm),
        grid=(m // tm, nj),
        in_specs=[
            pl.BlockSpec((tm, d), lambda i, j: (i, 0)),
            pl.BlockSpec((HALO_ROWS, d), lambda i, j: (jnp.maximum(i * halo_blocks - 1, 0), 0)),
            pl.BlockSpec((None, d, tn), lambda i, j: (layer, 0, j)),
            pl.BlockSpec((None, d, tn), lambda i, j: (layer, 0, nj + j)),
            pl.BlockSpec((None, FFN_CONV, tn), lambda i, j: (layer, 0, j)),
            pl.BlockSpec((None, FFN_CONV, tn), lambda i, j: (layer, 0, nj + j)),
            pl.BlockSpec((None, 1, tn), lambda i, j: (layer, 0, j)),
            pl.BlockSpec((None, 1, tn), lambda i, j: (layer, 0, nj + j)),
        ],
        out_specs=pl.BlockSpec((tm, tn), lambda i, j: (i, j)),
        out_shape=jax.ShapeDtypeStruct((m, d_ff), BF16),
        scratch_shapes=[pltpu.VMEM((tm + HALO_ROWS, d), BF16)],
        compiler_params=_params(2), name=name,
    )(h, h, w_up, w_up, conv_w, conv_w, conv_b, conv_b)


def _matmul_residual_kernel(x_ref, a_ref, w_ref, o_ref):
    o_ref[...] = x_ref[...] + _dot(a_ref[...], w_ref[...])


def matmul_residual(x, a, w, layer, *, tm, tn, name):
    m, n = x.shape
    k = a.shape[1]
    tm = min(tm, m)
    return pl.pallas_call(
        _matmul_residual_kernel,
        grid=(m // tm, n // tn),
        in_specs=[
            pl.BlockSpec((tm, tn), lambda i, j: (i, j)),
            pl.BlockSpec((tm, k), lambda i, j: (i, 0)),
            pl.BlockSpec((None, k, tn), lambda i, j: (layer, 0, j)),
        ],
        out_specs=pl.BlockSpec((tm, tn), lambda i, j: (i, j)),
        out_shape=jax.ShapeDtypeStruct((m, n), F32),
        compiler_params=_params(2), name=name,
    )(x, a, w)


def _final_norm_kernel(x_ref, g_ref, o_ref, *, tm):
    _rmsnorm_into(o_ref, 0, x_ref, g_ref[...], tm)


def final_norm(x, g, *, tm, name):
    m, d = x.shape
    tm = min(tm, m)
    return pl.pallas_call(
        functools.partial(_final_norm_kernel, tm=tm),
        grid=(m // tm,),
        in_specs=[pl.BlockSpec((tm, d), lambda i: (i, 0)), pl.BlockSpec((1, d), lambda i: (0, 0))],
        out_specs=pl.BlockSpec((tm, d), lambda i: (i, 0)),
        out_shape=jax.ShapeDtypeStruct((m, d), F32),
        compiler_params=_params(1), name=name,
    )(x, g)


def kernel(x, mem, mix_norm, w_in, gdn_conv, gdn_a_log, gdn_dt_bias, gdn_norm, w_out, xattn_norm,
           mem_norm, w_xq, w_xkv, w_xo, ffn_norm, w_up, ffn_conv, ffn_conv_bias, w_down, final_norm_g):
    batch, seq, d_model = x.shape
    n_mem = mem.shape[1]
    depth = w_in.shape[0]
    sup = min(GDN_SUPER, seq)
    n_gate = 2 * GDN_HEADS

    w_out_b = w_out.astype(BF16)
    w_down_b = w_down.astype(BF16)
    w_in_t = jnp.swapaxes(w_in, 1, 2)
    row3 = lambda a: a.reshape(depth, 1, a.shape[-1])
    mix_g, xat_g, mem_g, ffn_g = row3(mix_norm), row3(xattn_norm), row3(mem_norm), row3(ffn_norm)
    gdn_g = row3(gdn_norm)
    conv_b = row3(ffn_conv_bias)

    xf = x.reshape(batch * seq, d_model)
    memf = mem.reshape(batch * n_mem, d_model)

    t = TILES
    for l in range(depth):
        sb_qkv, gdn_qkvz, gates = in_projection(xf, mix_g, w_in_t, l, 3 * SB_WIDTH, 4 * GDN_WIDTH,
                                                **t["in_proj"], name=f"inproj_{l}")
        sb_out = sb_attention(sb_qkv, batch, seq, name=f"sb_attn_{l}")
        ba = gates.reshape(batch, seq, n_gate).transpose(0, 2, 1)
        ba_rows = ba.reshape(batch, n_gate, seq // sup, sup)
        ba_cols = ba_rows.transpose(0, 1, 3, 2)
        gdn_out = gated_deltanet(gdn_qkvz, ba_rows, ba_cols, gdn_conv, gdn_a_log, gdn_dt_bias, gdn_g,
                                 l, batch, seq, name=f"gdn_{l}")
        xf = proj2_residual(xf, sb_out, gdn_out, w_out_b, l, **t["out_proj"], name=f"outproj_{l}")

        kv = norm_matmul(memf, mem_g, w_xkv, l, BF16, **t["mem_kv"], name=f"mem_kv_{l}")
        xf, h_ffn = cross_attention_block(xf, xat_g, w_xq, kv, w_xo, ffn_g, l, seq, n_mem,
                                          **t["xattn"], name=f"xattn_{l}")

        act = ffn_up(h_ffn, w_up, ffn_conv, conv_b, l, seq, **t["ffn_up"], name=f"ffn_up_{l}")
        xf = matmul_residual(xf, act, w_down_b, l, **t["ffn_down"], name=f"ffn_down_{l}")

    out = final_norm(xf, final_norm_g.reshape(1, d_model), **t["final_norm"], name="final_norm")
    return out.reshape(batch, seq, d_model)
```

```python
import functools
import math

import jax
import jax.numpy as jnp
from jax import lax
from jax.experimental import pallas as pl
from jax.experimental.pallas import tpu as pltpu

F32 = jnp.float32
BF16 = jnp.bfloat16

HEAD_DIM = 128
SB_HEADS = 8
SB_GROUP = 4
GDN_HEADS = 8
SB_WIDTH = SB_HEADS * HEAD_DIM
GDN_WIDTH = GDN_HEADS * HEAD_DIM
SHORT_CONV = 4
CHUNK = 64
GDN_SUPER = 256
GDN_GROUP = 4
X_HEADS = 4
X_WIDTH = X_HEADS * HEAD_DIM
FFN_CONV = 3
EPS = 1e-6

V7X_VMEM_BYTES = 64 * 1024 * 1024
VMEM_LIMIT_BYTES = V7X_VMEM_BYTES - 8 * 1024 * 1024
TILES = {
    "in_proj": dict(tm=2048, tn=512),
    "out_proj": dict(tm=1024, tn=1024),
    "mem_kv": dict(tm=512, tn=512),
    "xattn": dict(tm=512),
    "ffn_up": dict(tm=1024, tn=512),
    "ffn_down": dict(tm=1024, tn=512),
    "final_norm": dict(tm=512),
}
DOT_ROWS = 1024
NORM_ROWS = 64
HALO_ROWS = 16
COPY_ROWS = 128


def _params(n_axes):
    return pltpu.CompilerParams(
        dimension_semantics=("arbitrary",) * n_axes, vmem_limit_bytes=VMEM_LIMIT_BYTES)


def _rmsnorm(x, g):
    ms = jnp.mean(x * x, axis=-1, keepdims=True)
    return x * lax.rsqrt(ms + EPS) * g


def _rmsnorm_into(dst_ref, dst_row0, x_ref, g, n_rows, scale=None):
    step = min(NORM_ROWS, n_rows)
    dst_align = math.gcd(step, dst_row0) if dst_row0 else step

    def body(r, carry):
        src = pl.ds(pl.multiple_of(r * step, step), step)
        dst = pl.ds(pl.multiple_of(dst_row0 + r * step, dst_align), step)
        y = _rmsnorm(x_ref[src, :], g)
        if scale is not None:
            y = y * scale
        dst_ref[dst, :] = y.astype(dst_ref.dtype)
        return carry

    lax.fori_loop(0, n_rows // step, body, 0)


def _aligned(start, multiple):
    return start if isinstance(start, int) else pl.multiple_of(start, multiple)


def _interleave(*gens):
    live = list(gens)
    while live:
        for g in list(live):
            if next(g, live) is live:
                live.remove(g)


def _sigmoid(x):
    return 1.0 / (1.0 + jnp.exp(-x))


def _silu(x):
    return x * _sigmoid(x)


def _softplus(x):
    return jnp.maximum(x, 0.0) + jnp.log1p(jnp.exp(-jnp.abs(x)))


def _dot(a, b):
    return lax.dot_general(a, b, (((1,), (0,)), ((), ())), preferred_element_type=F32)


def _dot_nt(a, b):
    return lax.dot_general(a, b, (((1,), (1,)), ((), ())), preferred_element_type=F32)


def _dot_tn(a, b):
    return lax.dot_general(a, b, (((0,), (0,)), ((), ())), preferred_element_type=F32)


def _norm_matmul_kernel(x_ref, g_ref, w_ref, o_ref, h_ref, *, tm):
    @pl.when(pl.program_id(1) == 0)
    def _():
        _rmsnorm_into(h_ref, 0, x_ref, g_ref[...], tm)

    o_ref[...] = _dot(h_ref[...], w_ref[...]).astype(o_ref.dtype)


def norm_matmul(x, g, w, layer, out_dtype, *, tm, tn, name):
    m, k = x.shape
    n = w.shape[-1]
    tm = min(tm, m)
    assert m % tm == 0 and n % tn == 0
    return pl.pallas_call(
        functools.partial(_norm_matmul_kernel, tm=tm),
        grid=(m // tm, n // tn),
        in_specs=[
            pl.BlockSpec((tm, k), lambda i, j: (i, 0)),
            pl.BlockSpec((None, 1, k), lambda i, j: (layer, 0, 0)),
            pl.BlockSpec((None, k, tn), lambda i, j: (layer, 0, j)),
        ],
        out_specs=pl.BlockSpec((tm, tn), lambda i, j: (i, j)),
        out_shape=jax.ShapeDtypeStruct((m, n), out_dtype),
        scratch_shapes=[pltpu.VMEM((tm, k), BF16)],
        compiler_params=_params(2), name=name,
    )(x, g, w)


def _in_proj_kernel(x_ref, g_ref, wt_ref, wst_ref, osb_ref, ogdn_ref, ogate_ref, h_ref, *, tm, n_sb):
    j = pl.program_id(1)

    @pl.when(j == 0)
    def _():
        _rmsnorm_into(h_ref, 0, x_ref, g_ref[...], tm)
        ogate_ref[...] = _dot_nt(h_ref[...], wst_ref[...])

    def project(o_ref):
        for r0 in range(0, tm, DOT_ROWS):
            rows = slice(r0, min(r0 + DOT_ROWS, tm))
            o_ref[rows, :] = _dot_nt(h_ref[rows, :], wt_ref[...]).astype(o_ref.dtype)

    @pl.when(j < n_sb)
    def _():
        project(osb_ref)

    @pl.when(j >= n_sb)
    def _():
        project(ogdn_ref)


def in_projection(x, g, w_t, layer, n_sb_cols, n_gdn_cols, *, tm, tn, name):
    m, k = x.shape
    tm = min(tm, m)
    n_main = n_sb_cols + n_gdn_cols
    n_gate = w_t.shape[1] - n_main
    assert m % tm == 0 and n_sb_cols % tn == 0 and n_gdn_cols % tn == 0
    assert n_gate > 0 and n_main % n_gate == 0 and n_gate % 8 == 0
    n_sb, n_gdn = n_sb_cols // tn, n_gdn_cols // tn
    return pl.pallas_call(
        functools.partial(_in_proj_kernel, tm=tm, n_sb=n_sb),
        grid=(m // tm, n_sb + n_gdn),
        in_specs=[
            pl.BlockSpec((tm, k), lambda i, j: (i, 0), pipeline_mode=pl.Buffered(1)),
            pl.BlockSpec((None, 1, k), lambda i, j: (layer, 0, 0)),
            pl.BlockSpec((None, tn, k), lambda i, j: (layer, j, 0)),
            pl.BlockSpec((None, n_gate, k), lambda i, j: (layer, n_main // n_gate, 0)),
        ],
        out_specs=[
            pl.BlockSpec((tm, tn), lambda i, j: (i, jnp.minimum(j, n_sb - 1))),
            pl.BlockSpec((tm, tn), lambda i, j: (i, jnp.maximum(j - n_sb, 0))),
            pl.BlockSpec((tm, n_gate), lambda i, j: (i, 0)),
        ],
        out_shape=[jax.ShapeDtypeStruct((m, n_sb_cols), BF16),
                   jax.ShapeDtypeStruct((m, n_gdn_cols), F32),
                   jax.ShapeDtypeStruct((m, n_gate), F32)],
        scratch_shapes=[pltpu.VMEM((tm, k), BF16)],
        compiler_params=_params(2), name=name,
    )(x, g, w_t, w_t)


def _sb_attn_kernel(q_ref, k_ref, v_ref, o_ref, acc_s, carry_s, *, seq, tile, sub):
    scale = HEAD_DIM ** -0.5
    n_sub = tile // sub
    r = lax.broadcasted_iota(jnp.int32, (sub, sub), 0)
    c = lax.broadcasted_iota(jnp.int32, (sub, sub), 1)
    suffix = jnp.where(r > c, 1.0, 0.0).astype(BF16)
    causal = c < r
    sign_bit = jnp.uint32(0x80000000)

    heads = range(SB_GROUP)
    lanes = lambda hh: slice(hh * HEAD_DIM, (hh + 1) * HEAD_DIM)

    def block_rounds(hh, q, qrows, kb, n_groups, diag):
        keys = pl.ds(pl.multiple_of(kb * tile, tile), n_groups * sub)
        z = _dot_nt(q[qrows, :], k_ref[keys, lanes(hh)]) * scale
        yield
        neg_abs = lax.bitcast_convert_type(lax.bitcast_convert_type(z, jnp.uint32) | sign_bit, F32)
        log_beta = jnp.minimum(z, 0.0) - jnp.log(1.0 + jnp.exp(neg_abs))
        log_stay = log_beta - z
        laters = [None] * n_groups
        run = None if diag else carry_s[hh, qrows, :]
        for s in reversed(range(n_groups)):
            ls = log_stay[:, s * sub:(s + 1) * sub]
            if diag and s == n_groups - 1:
                ls = jnp.where(causal, ls, 0.0)
            cum = _dot(ls.astype(BF16), suffix)
            laters[s] = cum if run is None else cum + run
            tot = jnp.sum(ls, axis=-1, keepdims=True)
            run = tot if run is None else run + tot
        carry_s[hh, qrows, :] = run
        yield
        w = jnp.exp(log_beta + jnp.concatenate(laters, axis=1))
        if diag:
            own = jnp.where(causal, w[:, (n_groups - 1) * sub:], 0.0)
            w = own if n_groups == 1 else jnp.concatenate([w[:, :(n_groups - 1) * sub], own], axis=1)
        pv = _dot(w.astype(BF16), v_ref[keys, lanes(hh)])
        if diag:
            acc_s[hh, qrows, :] = pv
        else:
            acc_s[hh, qrows, :] += pv
        yield

    def q_block(qi, c):
        qrows = pl.ds(pl.multiple_of(qi * tile, tile), tile)
        qs = [q_ref[qrows, lanes(hh)] for hh in heads]
        _interleave(*[block_rounds(hh, qs[hh], slice(g * sub, (g + 1) * sub), qi, g + 1, True)
                      for g in range(n_sub) for hh in heads])

        def k_step(s, c2):
            _interleave(*[block_rounds(hh, qs[hh], slice(0, tile), qi - 1 - s, n_sub, False) for hh in heads])
            return c2

        lax.fori_loop(0, qi, k_step, 0)
        for hh in heads:
            o_ref[qrows, lanes(hh)] = acc_s[hh].astype(o_ref.dtype)
        return c

    lax.fori_loop(0, seq // tile, q_block, 0)


def sb_attention(qkv, batch, seq, *, tile=512, sub=256, name):
    tile = min(tile, seq)
    sub = min(sub, tile)
    n_groups = SB_HEADS // SB_GROUP
    width = SB_GROUP * HEAD_DIM
    spec = lambda part: pl.BlockSpec((seq, width), lambda b, g: (b, part * n_groups + g))
    return pl.pallas_call(
        functools.partial(_sb_attn_kernel, seq=seq, tile=tile, sub=sub),
        grid=(batch, n_groups),
        in_specs=[spec(0), spec(1), spec(2)],
        out_specs=pl.BlockSpec((seq, width), lambda b, g: (b, g)),
        out_shape=jax.ShapeDtypeStruct((batch * seq, SB_WIDTH), BF16),
        scratch_shapes=[pltpu.VMEM((SB_GROUP, tile, HEAD_DIM), F32), pltpu.VMEM((SB_GROUP, tile, 1), F32)],
        compiler_params=_params(2), name=name,
    )(qkv, qkv, qkv)


def _gdn_kernel(alog_ref, dtb_ref, xq_ref, xk_ref, xv_ref, z_ref, cwq_ref, cwk_ref, cwv_ref,
                brow_ref, arow_ref, bcol_ref, acol_ref, ng_ref, o_ref,
                u_s, wq_s, kt_s, at_s, gcr_s, gcol_s, *, layer, seq):
    sup = min(GDN_SUPER, seq)
    per = sup // CHUNK
    n_sup = seq // sup
    group = pl.program_id(1)
    lanes = lambda hh: slice(hh * HEAD_DIM, (hh + 1) * HEAD_DIM)

    ri = lax.broadcasted_iota(jnp.int32, (sup, sup), 0)
    ci = lax.broadcasted_iota(jnp.int32, (sup, sup), 1)
    same = (ri // CHUNK) == (ci // CHUNK)
    in_lower = jnp.logical_and(same, ri >= ci)
    neg_inf_mask = jnp.where(in_lower, 0.0, -jnp.inf)
    neg_strict = jnp.where(jnp.logical_and(same, ri > ci), -1.0, 0.0)
    tri_lower = jnp.where(in_lower, 1.0, 0.0)
    tri_upper = jnp.where(jnp.logical_and(same, ri <= ci), 1.0, 0.0)
    block_ones = jnp.where(same, 1.0, 0.0)
    sup_lane = lax.broadcasted_iota(jnp.int32, (sup, n_sup), 1)

    def l2norm(y):
        return y * lax.rsqrt(jnp.sum(y * y, axis=-1, keepdims=True) + EPS)

    def conv_silu(x_ref, cw_ref, hh, sc):
        cw = cw_ref[:, lanes(hh)]
        xb = x_ref[pl.ds(_aligned(sc * sup, sup), sup), lanes(hh)]
        if isinstance(sc, int) and sc == 0:
            prev = jnp.zeros((8, HEAD_DIM), F32)
        else:
            prev = x_ref[pl.ds(pl.multiple_of(jnp.maximum(sc * sup - 8, 0), 8), 8), lanes(hh)]
            prev = jnp.where(sc > 0, prev, 0.0)
        xx = jnp.concatenate([prev, xb], axis=0)
        y = cw[SHORT_CONV - 1:SHORT_CONV, :] * xx[8:, :]
        for tap in range(SHORT_CONV - 1):
            shift = SHORT_CONV - 1 - tap
            y = y + cw[tap:tap + 1, :] * xx[8 - shift:8 - shift + sup, :]
        return _silu(y)

    hdot = functools.partial(jnp.dot, precision=lax.Precision.HIGHEST, preferred_element_type=F32)
    for hh in range(GDN_GROUP):
        head = group * GDN_GROUP + hh
        neg_a = -jnp.exp(jnp.full((1, 1), alog_ref[layer, head], F32))
        dtb = dtb_ref[layer, head]
        g_row = neg_a * _softplus(arow_ref[hh] + dtb)
        g_col = neg_a * _softplus(acol_ref[hh] + dtb)
        gcr_s[hh] = hdot(g_row, tri_upper)
        gcol_s[hh, 0] = _sigmoid(bcol_ref[hh])
        gcol_s[hh, 1] = hdot(tri_lower, g_col)
        gcol_s[hh, 2] = hdot(block_ones, g_col)

    heads = range(GDN_GROUP)
    n_sq = max(CHUNK.bit_length() - 2, 0)

    def prep_rounds(sc, slot):
        st = [dict() for _ in heads]
        for hh in heads:
            s = st[hh]
            sel = sup_lane == sc
            pick = lambda kind: jnp.sum(jnp.where(sel, gcol_s[hh, kind], 0.0), axis=1, keepdims=True)
            bcol, gcol, glast = pick(0), pick(1), pick(2)
            grow = gcr_s[hh, pl.ds(sc, 1), :]
            qc = l2norm(conv_silu(xq_ref, cwq_ref, hh, sc)) * (HEAD_DIM ** -0.5)
            kc = l2norm(conv_silu(xk_ref, cwk_ref, hh, sc))
            s["decay"] = jnp.exp((gcol - grow) + neg_inf_mask)
            kb = kc * bcol
            egc = jnp.exp(gcol)
            s["rhs"] = jnp.concatenate([conv_silu(xv_ref, cwv_ref, hh, sc) * bcol, kb * egc], axis=1)
            s["kb"], s["kc"], s["qc"] = kb.astype(BF16), kc.astype(BF16), qc.astype(BF16)
            qd = (qc * egc).astype(BF16)
            for p in range(per):
                blk = slice(p * CHUNK, (p + 1) * CHUNK)
                wq_s[slot, hh, p, CHUNK:, :] = qd[blk, :]
            kt_s[slot, hh] = (kc * jnp.exp(glast - gcol)).astype(BF16)
        yield
        for hh in heads:
            s = st[hh]
            s["y"] = _dot_nt(s["kb"], s["kc"]) * s["decay"] * neg_strict
            s["z"] = s["y"]
            attn = (_dot_nt(s["qc"], s["kc"]) * s["decay"]).astype(BF16)
            for p in range(per):
                blk = slice(p * CHUNK, (p + 1) * CHUNK)
                at_s[slot, hh, blk, :] = attn[blk, blk]
        yield
        for m in range(n_sq):
            for hh in heads:
                s = st[hh]
                yb = s["y"].astype(BF16)
                if m > 0:
                    s["z"] = s["z"] + s["y"] + _dot(s["z"].astype(BF16), yb)
                s["y"] = _dot(yb, yb)
            yield
        for hh in heads:
            s = st[hh]
            s["z"] = s["z"] + s["y"] + _dot(s["z"].astype(BF16), s["y"].astype(BF16))
        yield
        for hh in heads:
            s = st[hh]
            uw = s["rhs"] + _dot(s["z"].astype(BF16), s["rhs"].astype(BF16))
            u_s[slot, hh] = uw[:, :HEAD_DIM]
            w = uw[:, HEAD_DIM:].astype(BF16)
            for p in range(per):
                blk = slice(p * CHUNK, (p + 1) * CHUNK)
                wq_s[slot, hh, p, :CHUNK, :] = w[blk, :]
        yield

    ng = ng_ref[...]

    def scan_rounds(sc, slot, states):
        chunk_decay = [jnp.exp(gcr_s[hh, pl.ds(sc, 1), :]) for hh in heads]
        for p in range(per):
            blk = slice(p * CHUNK, (p + 1) * CHUNK)
            rows = pl.ds(_aligned(sc * sup + p * CHUNK, CHUNK), CHUNK)
            vnb, qs = [None] * GDN_GROUP, [None] * GDN_GROUP
            for hh in heads:
                both = _dot(wq_s[slot, hh, p], states[hh].astype(BF16))
                vnb[hh] = (u_s[slot, hh, blk, :] - both[:CHUNK, :]).astype(BF16)
                qs[hh] = both[CHUNK:, :]
            yield
            for hh in heads:
                o = qs[hh] + _dot(at_s[slot, hh, blk, :], vnb[hh])
                last = p * CHUNK + CHUNK - 1
                cd = chunk_decay[hh][:, last:last + 1]
                states[hh] = states[hh] * cd + _dot_tn(kt_s[slot, hh, blk, :], vnb[hh])
                o_ref[rows, lanes(hh)] = (_rmsnorm(o, ng) * _silu(z_ref[rows, lanes(hh)])).astype(o_ref.dtype)
            yield

    def step(i, states):
        states = list(states)
        slot = lax.rem(i, 2)
        _interleave(prep_rounds(i, slot), scan_rounds(i - 1, 1 - slot, states))
        return tuple(states)

    _interleave(prep_rounds(0, 0))
    states = tuple(jnp.zeros((HEAD_DIM, HEAD_DIM), F32) for _ in heads)
    states = list(lax.fori_loop(1, n_sup, step, states))
    _interleave(scan_rounds(n_sup - 1, (n_sup - 1) % 2, states))


def gated_deltanet(qkvz, ba_rows, ba_cols, conv_w, a_log, dt_bias, norm_g, layer, batch, seq, *, name):
    sup = min(GDN_SUPER, seq)
    n_sup = seq // sup
    n_groups = GDN_HEADS // GDN_GROUP
    width = GDN_GROUP * HEAD_DIM
    col = lambda part: pl.BlockSpec((seq, width), lambda b, g: (b, part * n_groups + g))
    cw = lambda part: pl.BlockSpec((None, SHORT_CONV, width), lambda b, g: (layer, 0, part * n_groups + g))
    rows = lambda part: pl.BlockSpec((None, GDN_GROUP, n_sup, sup), lambda b, g: (b, part * n_groups + g, 0, 0))
    cols = lambda part: pl.BlockSpec((None, GDN_GROUP, sup, n_sup), lambda b, g: (b, part * n_groups + g, 0, 0))
    smem = pl.BlockSpec(memory_space=pltpu.SMEM)
    per_head = lambda shape, dt: pltpu.VMEM((GDN_GROUP,) + shape, dt)
    slots = lambda shape, dt: pltpu.VMEM((2, GDN_GROUP) + shape, dt)
    return pl.pallas_call(
        functools.partial(_gdn_kernel, layer=layer, seq=seq),
        grid=(batch, n_groups),
        in_specs=[smem, smem, col(0), col(1), col(2), col(3), cw(0), cw(1), cw(2),
                  rows(0), rows(1), cols(0), cols(1),
                  pl.BlockSpec((None, 1, HEAD_DIM), lambda b, g: (layer, 0, 0))],
        out_specs=pl.BlockSpec((seq, width), lambda b, g: (b, g)),
        out_shape=jax.ShapeDtypeStruct((batch * seq, GDN_WIDTH), BF16),
        scratch_shapes=[
            slots((sup, HEAD_DIM), F32),
            slots((sup // CHUNK, 2 * CHUNK, HEAD_DIM), BF16),
            slots((sup, HEAD_DIM), BF16),
            slots((sup, CHUNK), BF16),
            per_head((n_sup, sup), F32),
            per_head((3, sup, n_sup), F32),
        ],
        compiler_params=_params(2), name=name,
    )(a_log, dt_bias, qkvz, qkvz, qkvz, qkvz, conv_w, conv_w, conv_w,
      ba_rows, ba_rows, ba_cols, ba_cols, norm_g)


def _proj2_residual_kernel(x_ref, a1_ref, a2_ref, w1_ref, w2_ref, o_ref):
    o_ref[...] = x_ref[...] + _dot(a1_ref[...], w1_ref[...]) + _dot(a2_ref[...], w2_ref[...])


def proj2_residual(x, a1, a2, w, layer, *, tm, tn, name):
    m, n = x.shape
    k1, k2 = a1.shape[1], a2.shape[1]
    assert k1 == k2
    tm = min(tm, m)
    return pl.pallas_call(
        _proj2_residual_kernel,
        grid=(m // tm, n // tn),
        in_specs=[
            pl.BlockSpec((tm, tn), lambda i, j: (i, j)),
            pl.BlockSpec((tm, k1), lambda i, j: (i, 0)),
            pl.BlockSpec((tm, k2), lambda i, j: (i, 0)),
            pl.BlockSpec((None, k1, tn), lambda i, j: (layer, 0, j)),
            pl.BlockSpec((None, k2, tn), lambda i, j: (layer, 1, j)),
        ],
        out_specs=pl.BlockSpec((tm, tn), lambda i, j: (i, j)),
        out_shape=jax.ShapeDtypeStruct((m, n), F32),
        compiler_params=_params(2), name=name,
    )(x, a1, a2, w, w)


def _xattn_kernel(x_ref, g_ref, wq_ref, k_ref, v_ref, wo_ref, gn_ref, o_ref, hn_ref, h_s, a_s, *, tm):
    _rmsnorm_into(h_s, 0, x_ref, g_ref[...], tm)
    q = _dot(h_s[...], wq_ref[...]).astype(BF16)
    scale = HEAD_DIM ** -0.5
    cols = [slice(hd * HEAD_DIM, (hd + 1) * HEAD_DIM) for hd in range(X_HEADS)]
    logits = [_dot_nt(q[:, c], k_ref[:, c]) * scale for c in cols]
    probs = [jnp.exp(s - jnp.max(s, axis=-1, keepdims=True)) for s in logits]
    outs = [_dot(p.astype(BF16), v_ref[:, c]) for p, c in zip(probs, cols)]
    for p, o, c in zip(probs, outs, cols):
        a_s[:, c] = (o / jnp.sum(p, axis=-1, keepdims=True)).astype(BF16)
    o_ref[...] = x_ref[...] + _dot(a_s[...], wo_ref[...])
    _rmsnorm_into(hn_ref, 0, o_ref, gn_ref[...], tm)


def cross_attention_block(x, g, wq, kv, wo, g_next, layer, seq, n_mem, *, tm, name):
    m, d = x.shape
    tm = min(tm, seq)
    per_seq = seq // tm
    gain = pl.BlockSpec((None, 1, d), lambda i: (layer, 0, 0))
    rows = pl.BlockSpec((tm, d), lambda i: (i, 0))
    return pl.pallas_call(
        functools.partial(_xattn_kernel, tm=tm),
        grid=(m // tm,),
        in_specs=[
            rows, gain,
            pl.BlockSpec((None, d, X_WIDTH), lambda i: (layer, 0, 0)),
            pl.BlockSpec((n_mem, X_WIDTH), lambda i: (i // per_seq, 0)),
            pl.BlockSpec((n_mem, X_WIDTH), lambda i: (i // per_seq, 1)),
            pl.BlockSpec((None, X_WIDTH, d), lambda i: (layer, 0, 0)),
            gain,
        ],
        out_specs=[rows, rows],
        out_shape=[jax.ShapeDtypeStruct((m, d), F32), jax.ShapeDtypeStruct((m, d), BF16)],
        scratch_shapes=[pltpu.VMEM((tm, d), BF16), pltpu.VMEM((tm, X_WIDTH), BF16)],
        compiler_params=_params(1), name=name,
    )(x, g, wq, kv, kv, wo, g_next)


def _ffn_up_kernel(h_ref, halo_ref, wg_ref, wu_ref, cg_ref, cu_ref, bg_ref, bu_ref,
                   o_ref, h_s, *, tm, tiles_per_seq):
    i = pl.program_id(0)

    @pl.when(pl.program_id(1) == 0)
    def _():
        keep = jnp.where(i % tiles_per_seq == 0, 0.0, 1.0).astype(h_s.dtype)
        h_s[:HALO_ROWS, :] = halo_ref[...] * keep
        step = min(COPY_ROWS, tm)

        def body(r, carry):
            src = pl.ds(pl.multiple_of(r * step, step), step)
            dst = pl.ds(pl.multiple_of(HALO_ROWS + r * step, HALO_ROWS), step)
            h_s[dst, :] = h_ref[src, :]
            return carry

        lax.fori_loop(0, tm // step, body, 0)

    def conv(y, cw_ref, b_ref):
        cw = cw_ref[...]
        out = cw[FFN_CONV - 1:FFN_CONV, :] * y[HALO_ROWS:, :]
        for tap in range(FFN_CONV - 1):
            shift = FFN_CONV - 1 - tap
            out = out + cw[tap:tap + 1, :] * y[HALO_ROWS - shift:HALO_ROWS - shift + tm, :]
        return out + b_ref[...]

    h = h_s[...]
    gate = conv(_dot(h, wg_ref[...]), cg_ref, bg_ref)
    up = conv(_dot(h, wu_ref[...]), cu_ref, bu_ref)
    o_ref[...] = (_silu(gate) * up).astype(o_ref.dtype)


def ffn_up(h, w_up, conv_w, conv_b, layer, seq, *, tm, tn, name):
    m, d = h.shape
    d_ff = w_up.shape[-1] // 2
    tm = min(tm, seq)
    assert d_ff % tn == 0 and seq % tm == 0 and tm % HALO_ROWS == 0
    nj = d_ff // tn
    halo_blocks = tm // HALO_ROWS
    return pl.pallas_call(
        functools.partial(_ffn_up_kernel, tm=tm, tiles_per_seq=seq // tm),
        grid=(m // tm, nj),
        in_specs=[
            pl.BlockSpec((tm, d), lambda i, j: (i, 0)),
            pl.BlockSpec((HALO_ROWS, d), lambda i, j: (jnp.maximum(i * halo_blocks - 1, 0), 0)),
            pl.BlockSpec((None, d, tn), lambda i, j: (layer, 0, j)),
            pl.BlockSpec((None, d, tn), lambda i, j: (layer, 0, nj + j)),
            pl.BlockSpec((None, FFN_CONV, tn), lambda i, j: (layer, 0, j)),
            pl.BlockSpec((None, FFN_CONV, tn), lambda i, j: (layer, 0, nj + j)),
            pl.BlockSpec((None, 1, tn), lambda i, j: (layer, 0, j)),
            pl.BlockSpec((None, 1, tn), lambda i, j: (layer, 0, nj + j)),
        ],
        out_specs=pl.BlockSpec((tm, tn), lambda i, j: (i, j)),
        out_shape=jax.ShapeDtypeStruct((m, d_ff), BF16),
        scratch_shapes=[pltpu.VMEM((tm + HALO_ROWS, d), BF16)],
        compiler_params=_params(2), name=name,
    )(h, h, w_up, w_up, conv_w, conv_w, conv_b, conv_b)


def _matmul_residual_kernel(x_ref, a_ref, w_ref, o_ref):
    o_ref[...] = x_ref[...] + _dot(a_ref[...], w_ref[...])


def matmul_residual(x, a, w, layer, *, tm, tn, name):
    m, n = x.shape
    k = a.shape[1]
    tm = min(tm, m)
    return pl.pallas_call(
        _matmul_residual_kernel,
        grid=(m // tm, n // tn),
        in_specs=[
            pl.BlockSpec((tm, tn), lambda i, j: (i, j)),
            pl.BlockSpec((tm, k), lambda i, j: (i, 0)),
            pl.BlockSpec((None, k, tn), lambda i, j: (layer, 0, j)),
        ],
        out_specs=pl.BlockSpec((tm, tn), lambda i, j: (i, j)),
        out_shape=jax.ShapeDtypeStruct((m, n), F32),
        compiler_params=_params(2), name=name,
    )(x, a, w)


def _final_norm_kernel(x_ref, g_ref, o_ref, *, tm):
    _rmsnorm_into(o_ref, 0, x_ref, g_ref[...], tm)


def final_norm(x, g, *, tm, name):
    m, d = x.shape
    tm = min(tm, m)
    return pl.pallas_call(
        functools.partial(_final_norm_kernel, tm=tm),
        grid=(m // tm,),
        in_specs=[pl.BlockSpec((tm, d), lambda i: (i, 0)), pl.BlockSpec((1, d), lambda i: (0, 0))],
        out_specs=pl.BlockSpec((tm, d), lambda i: (i, 0)),
        out_shape=jax.ShapeDtypeStruct((m, d), F32),
        compiler_params=_params(1), name=name,
    )(x, g)


def kernel(x, mem, mix_norm, w_in, gdn_conv, gdn_a_log, gdn_dt_bias, gdn_norm, w_out, xattn_norm,
           mem_norm, w_xq, w_xkv, w_xo, ffn_norm, w_up, ffn_conv, ffn_conv_bias, w_down, final_norm_g):
    batch, seq, d_model = x.shape
    n_mem = mem.shape[1]
    depth = w_in.shape[0]
    sup = min(GDN_SUPER, seq)
    n_gate = 2 * GDN_HEADS

    w_out_b = w_out.astype(BF16)
    w_down_b = w_down.astype(BF16)
    w_in_t = jnp.swapaxes(w_in, 1, 2)
    row3 = lambda a: a.reshape(depth, 1, a.shape[-1])
    mix_g, xat_g, mem_g, ffn_g = row3(mix_norm), row3(xattn_norm), row3(mem_norm), row3(ffn_norm)
    gdn_g = row3(gdn_norm)
    conv_b = row3(ffn_conv_bias)

    xf = x.reshape(batch * seq, d_model)
    memf = mem.reshape(batch * n_mem, d_model)

    t = TILES
    for l in range(depth):
        sb_qkv, gdn_qkvz, gates = in_projection(xf, mix_g, w_in_t, l, 3 * SB_WIDTH, 4 * GDN_WIDTH,
                                                **t["in_proj"], name=f"inproj_{l}")
        sb_out = sb_attention(sb_qkv, batch, seq, name=f"sb_attn_{l}")
        ba = gates.reshape(batch, seq, n_gate).transpose(0, 2, 1)
        ba_rows = ba.reshape(batch, n_gate, seq // sup, sup)
        ba_cols = ba_rows.transpose(0, 1, 3, 2)
        gdn_out = gated_deltanet(gdn_qkvz, ba_rows, ba_cols, gdn_conv, gdn_a_log, gdn_dt_bias, gdn_g,
                                 l, batch, seq, name=f"gdn_{l}")
        xf = proj2_residual(xf, sb_out, gdn_out, w_out_b, l, **t["out_proj"], name=f"outproj_{l}")

        kv = norm_matmul(memf, mem_g, w_xkv, l, BF16, **t["mem_kv"], name=f"mem_kv_{l}")
        xf, h_ffn = cross_attention_block(xf, xat_g, w_xq, kv, w_xo, ffn_g, l, seq, n_mem,
                                          **t["xattn"], name=f"xattn_{l}")

        act = ffn_up(h_ffn, w_up, ffn_conv, conv_b, l, seq, **t["ffn_up"], name=f"ffn_up_{l}")
        xf = matmul_residual(xf, act, w_down_b, l, **t["ffn_down"], name=f"ffn_down_{l}")

    out = final_norm(xf, final_norm_g.reshape(1, d_model), **t["final_norm"], name="final_norm")
    return out.reshape(batch, seq, d_model)
```

```python
import functools
import math

import jax
import jax.numpy as jnp
from jax import lax
from jax.experimental import pallas as pl
from jax.experimental.pallas import tpu as pltpu

F32 = jnp.float32
BF16 = jnp.bfloat16

HEAD_DIM = 128
SB_HEADS = 8
SB_GROUP = 4
GDN_HEADS = 8
SB_WIDTH = SB_HEADS * HEAD_DIM
GDN_WIDTH = GDN_HEADS * HEAD_DIM
SHORT_CONV = 4
CHUNK = 64
GDN_SUPER = 256
GDN_GROUP = 4
X_HEADS = 4
X_WIDTH = X_HEADS * HEAD_DIM
FFN_CONV = 3
EPS = 1e-6
LOG2E = 1.4426950408889634

V7X_VMEM_BYTES = 64 * 1024 * 1024
VMEM_LIMIT_BYTES = V7X_VMEM_BYTES - 8 * 1024 * 1024
TILES = {
    "in_proj": dict(tm=2048, tn=512),
    "out_proj": dict(tm=1024, tn=1024),
    "mem_kv": dict(tm=512, tn=512),
    "xattn": dict(tm=512),
    "ffn_up": dict(tm=1024, tn=512),
    "ffn_down": dict(tm=1024, tn=512),
    "final_norm": dict(tm=512),
}
DOT_ROWS = 1024
NORM_ROWS = 64
HALO_ROWS = 16
COPY_ROWS = 128


def _params(n_axes):
    return pltpu.CompilerParams(
        dimension_semantics=("arbitrary",) * n_axes, vmem_limit_bytes=VMEM_LIMIT_BYTES)


def _rmsnorm(x, g):
    ms = jnp.mean(x * x, axis=-1, keepdims=True)
    return x * lax.rsqrt(ms + EPS) * g


def _rmsnorm_into(dst_ref, dst_row0, x_ref, g, n_rows, scale=None):
    step = min(NORM_ROWS, n_rows)
    dst_align = math.gcd(step, dst_row0) if dst_row0 else step

    def body(r, carry):
        src = pl.ds(pl.multiple_of(r * step, step), step)
        dst = pl.ds(pl.multiple_of(dst_row0 + r * step, dst_align), step)
        y = _rmsnorm(x_ref[src, :], g)
        if scale is not None:
            y = y * scale
        dst_ref[dst, :] = y.astype(dst_ref.dtype)
        return carry

    lax.fori_loop(0, n_rows // step, body, 0)


def _aligned(start, multiple):
    return start if isinstance(start, int) else pl.multiple_of(start, multiple)


def _interleave(*gens):
    live = list(gens)
    while live:
        for g in list(live):
            if next(g, live) is live:
                live.remove(g)


def _sigmoid(x):
    return 1.0 / (1.0 + jnp.exp(-x))


def _silu(x):
    return x * _sigmoid(x)


def _softplus(x):
    return jnp.maximum(x, 0.0) + jnp.log1p(jnp.exp(-jnp.abs(x)))


def _dot(a, b):
    return lax.dot_general(a, b, (((1,), (0,)), ((), ())), preferred_element_type=F32)


def _dot_nt(a, b):
    return lax.dot_general(a, b, (((1,), (1,)), ((), ())), preferred_element_type=F32)


def _dot_tn(a, b):
    return lax.dot_general(a, b, (((0,), (0,)), ((), ())), preferred_element_type=F32)


def _norm_matmul_kernel(x_ref, g_ref, w_ref, o_ref, h_ref, *, tm):
    @pl.when(pl.program_id(1) == 0)
    def _():
        _rmsnorm_into(h_ref, 0, x_ref, g_ref[...], tm)

    o_ref[...] = _dot(h_ref[...], w_ref[...]).astype(o_ref.dtype)


def norm_matmul(x, g, w, layer, out_dtype, *, tm, tn, name):
    m, k = x.shape
    n = w.shape[-1]
    tm = min(tm, m)
    assert m % tm == 0 and n % tn == 0
    return pl.pallas_call(
        functools.partial(_norm_matmul_kernel, tm=tm),
        grid=(m // tm, n // tn),
        in_specs=[
            pl.BlockSpec((tm, k), lambda i, j: (i, 0)),
            pl.BlockSpec((None, 1, k), lambda i, j: (layer, 0, 0)),
            pl.BlockSpec((None, k, tn), lambda i, j: (layer, 0, j)),
        ],
        out_specs=pl.BlockSpec((tm, tn), lambda i, j: (i, j)),
        out_shape=jax.ShapeDtypeStruct((m, n), out_dtype),
        scratch_shapes=[pltpu.VMEM((tm, k), BF16)],
        compiler_params=_params(2), name=name,
    )(x, g, w)


def _in_proj_kernel(x_ref, g_ref, wt_ref, wst_ref, osb_ref, ogdn_ref, ogate_ref, h_ref, *, tm, n_sb):
    j = pl.program_id(1)

    @pl.when(j == 0)
    def _():
        _rmsnorm_into(h_ref, 0, x_ref, g_ref[...], tm)
        ogate_ref[...] = _dot_nt(h_ref[...], wst_ref[...])

    def project(o_ref):
        for r0 in range(0, tm, DOT_ROWS):
            rows = slice(r0, min(r0 + DOT_ROWS, tm))
            o_ref[rows, :] = _dot_nt(h_ref[rows, :], wt_ref[...]).astype(o_ref.dtype)

    @pl.when(j < n_sb)
    def _():
        project(osb_ref)

    @pl.when(j >= n_sb)
    def _():
        project(ogdn_ref)


def in_projection(x, g, w_t, layer, n_sb_cols, n_gdn_cols, *, tm, tn, name):
    m, k = x.shape
    tm = min(tm, m)
    n_main = n_sb_cols + n_gdn_cols
    n_gate = w_t.shape[1] - n_main
    assert m % tm == 0 and n_sb_cols % tn == 0 and n_gdn_cols % tn == 0
    assert n_gate > 0 and n_main % n_gate == 0 and n_gate % 8 == 0
    n_sb, n_gdn = n_sb_cols // tn, n_gdn_cols // tn
    return pl.pallas_call(
        functools.partial(_in_proj_kernel, tm=tm, n_sb=n_sb),
        grid=(m // tm, n_sb + n_gdn),
        in_specs=[
            pl.BlockSpec((tm, k), lambda i, j: (i, 0), pipeline_mode=pl.Buffered(1)),
            pl.BlockSpec((None, 1, k), lambda i, j: (layer, 0, 0)),
            pl.BlockSpec((None, tn, k), lambda i, j: (layer, j, 0)),
            pl.BlockSpec((None, n_gate, k), lambda i, j: (layer, n_main // n_gate, 0)),
        ],
        out_specs=[
            pl.BlockSpec((tm, tn), lambda i, j: (i, jnp.minimum(j, n_sb - 1))),
            pl.BlockSpec((tm, tn), lambda i, j: (i, jnp.maximum(j - n_sb, 0))),
            pl.BlockSpec((tm, n_gate), lambda i, j: (i, 0)),
        ],
        out_shape=[jax.ShapeDtypeStruct((m, n_sb_cols), BF16),
                   jax.ShapeDtypeStruct((m, n_gdn_cols), F32),
                   jax.ShapeDtypeStruct((m, n_gate), F32)],
        scratch_shapes=[pltpu.VMEM((tm, k), BF16)],
        compiler_params=_params(2), name=name,
    )(x, g, w_t, w_t)


def _sb_attn_kernel(q_ref, k_ref, v_ref, o_ref, acc_s, carry_s, *, seq, tile, sub):
    scale = HEAD_DIM ** -0.5
    n_sub = tile // sub
    r = lax.broadcasted_iota(jnp.int32, (sub, sub), 0)
    c = lax.broadcasted_iota(jnp.int32, (sub, sub), 1)
    suffix = jnp.where(r > c, 1.0, 0.0).astype(BF16)
    causal = c < r
    sign_bit = jnp.uint32(0x80000000)

    heads = range(SB_GROUP)
    lanes = lambda hh: slice(hh * HEAD_DIM, (hh + 1) * HEAD_DIM)

    def block_rounds(hh, q, qrows, kb, n_groups, diag):
        keys = pl.ds(pl.multiple_of(kb * tile, tile), n_groups * sub)
        z = _dot_nt(q[qrows, :], k_ref[keys, lanes(hh)]) * (scale * LOG2E)
        yield
        neg_abs = lax.bitcast_convert_type(lax.bitcast_convert_type(z, jnp.uint32) | sign_bit, F32)
        log_beta = jnp.minimum(z, 0.0) - jnp.log(1.0 + jnp.exp2(neg_abs)) * LOG2E
        log_stay = log_beta - z
        laters = [None] * n_groups
        run = None if diag else carry_s[hh, qrows, :]
        for s in reversed(range(n_groups)):
            ls = log_stay[:, s * sub:(s + 1) * sub]
            if diag and s == n_groups - 1:
                ls = jnp.where(causal, ls, 0.0)
            cum = _dot(ls.astype(BF16), suffix)
            laters[s] = cum if run is None else cum + run
            tot = jnp.sum(ls, axis=-1, keepdims=True)
            run = tot if run is None else run + tot
        carry_s[hh, qrows, :] = run
        yield
        w = jnp.exp2(log_beta + jnp.concatenate(laters, axis=1))
        if diag:
            own = jnp.where(causal, w[:, (n_groups - 1) * sub:], 0.0)
            w = own if n_groups == 1 else jnp.concatenate([w[:, :(n_groups - 1) * sub], own], axis=1)
        pv = _dot(w.astype(BF16), v_ref[keys, lanes(hh)])
        if diag:
            acc_s[hh, qrows, :] = pv
        else:
            acc_s[hh, qrows, :] += pv
        yield

    def q_block(qi, c):
        qrows = pl.ds(pl.multiple_of(qi * tile, tile), tile)
        qs = [q_ref[qrows, lanes(hh)] for hh in heads]
        _interleave(*[block_rounds(hh, qs[hh], slice(g * sub, (g + 1) * sub), qi, g + 1, True)
                      for g in range(n_sub) for hh in heads])

        def k_step(s, c2):
            _interleave(*[block_rounds(hh, qs[hh], slice(0, tile), qi - 1 - s, n_sub, False) for hh in heads])
            return c2

        lax.fori_loop(0, qi, k_step, 0)
        for hh in heads:
            o_ref[qrows, lanes(hh)] = acc_s[hh].astype(o_ref.dtype)
        return c

    lax.fori_loop(0, seq // tile, q_block, 0)


def sb_attention(qkv, batch, seq, *, tile=512, sub=256, name):
    tile = min(tile, seq)
    sub = min(sub, tile)
    n_groups = SB_HEADS // SB_GROUP
    width = SB_GROUP * HEAD_DIM
    spec = lambda part: pl.BlockSpec((seq, width), lambda b, g: (b, part * n_groups + g))
    return pl.pallas_call(
        functools.partial(_sb_attn_kernel, seq=seq, tile=tile, sub=sub),
        grid=(batch, n_groups),
        in_specs=[spec(0), spec(1), spec(2)],
        out_specs=pl.BlockSpec((seq, width), lambda b, g: (b, g)),
        out_shape=jax.ShapeDtypeStruct((batch * seq, SB_WIDTH), BF16),
        scratch_shapes=[pltpu.VMEM((SB_GROUP, tile, HEAD_DIM), F32), pltpu.VMEM((SB_GROUP, tile, 1), F32)],
        compiler_params=_params(2), name=name,
    )(qkv, qkv, qkv)


def _gdn_kernel(alog_ref, dtb_ref, xq_ref, xk_ref, xv_ref, z_ref, cwq_ref, cwk_ref, cwv_ref,
                brow_ref, arow_ref, bcol_ref, acol_ref, ng_ref, o_ref,
                u_s, wq_s, kt_s, at_s, gcr_s, gcol_s, *, layer, seq):
    sup = min(GDN_SUPER, seq)
    per = sup // CHUNK
    n_sup = seq // sup
    group = pl.program_id(1)
    lanes = lambda hh: slice(hh * HEAD_DIM, (hh + 1) * HEAD_DIM)

    ri = lax.broadcasted_iota(jnp.int32, (sup, sup), 0)
    ci = lax.broadcasted_iota(jnp.int32, (sup, sup), 1)
    same = (ri // CHUNK) == (ci // CHUNK)
    in_lower = jnp.logical_and(same, ri >= ci)
    neg_inf_mask = jnp.where(in_lower, 0.0, -jnp.inf)
    neg_strict = jnp.where(jnp.logical_and(same, ri > ci), -1.0, 0.0)
    tri_lower = jnp.where(in_lower, 1.0, 0.0)
    tri_upper = jnp.where(jnp.logical_and(same, ri <= ci), 1.0, 0.0)
    block_ones = jnp.where(same, 1.0, 0.0)
    sup_lane = lax.broadcasted_iota(jnp.int32, (sup, n_sup), 1)

    def l2norm(y):
        return y * lax.rsqrt(jnp.sum(y * y, axis=-1, keepdims=True) + EPS)

    def conv_silu(x_ref, cw_ref, hh, sc):
        cw = cw_ref[:, lanes(hh)]
        xb = x_ref[pl.ds(_aligned(sc * sup, sup), sup), lanes(hh)]
        if isinstance(sc, int) and sc == 0:
            prev = jnp.zeros((8, HEAD_DIM), F32)
        else:
            prev = x_ref[pl.ds(pl.multiple_of(jnp.maximum(sc * sup - 8, 0), 8), 8), lanes(hh)]
            prev = jnp.where(sc > 0, prev, 0.0)
        xx = jnp.concatenate([prev, xb], axis=0)
        y = cw[SHORT_CONV - 1:SHORT_CONV, :] * xx[8:, :]
        for tap in range(SHORT_CONV - 1):
            shift = SHORT_CONV - 1 - tap
            y = y + cw[tap:tap + 1, :] * xx[8 - shift:8 - shift + sup, :]
        return _silu(y)

    hdot = functools.partial(jnp.dot, precision=lax.Precision.HIGHEST, preferred_element_type=F32)
    for hh in range(GDN_GROUP):
        head = group * GDN_GROUP + hh
        neg_a = -jnp.exp(jnp.full((1, 1), alog_ref[layer, head], F32))
        dtb = dtb_ref[layer, head]
        g_row = neg_a * _softplus(arow_ref[hh] + dtb)
        g_col = neg_a * _softplus(acol_ref[hh] + dtb)
        gcr_s[hh] = hdot(g_row, tri_upper)
        gcol_s[hh, 0] = _sigmoid(bcol_ref[hh])
        gcol_s[hh, 1] = hdot(tri_lower, g_col)
        gcol_s[hh, 2] = hdot(block_ones, g_col)

    heads = range(GDN_GROUP)
    n_sq = max(CHUNK.bit_length() - 2, 0)

    def prep_rounds(sc, slot):
        st = [dict() for _ in heads]
        for hh in heads:
            s = st[hh]
            sel = sup_lane == sc
            pick = lambda kind: jnp.sum(jnp.where(sel, gcol_s[hh, kind], 0.0), axis=1, keepdims=True)
            bcol, gcol, glast = pick(0), pick(1), pick(2)
            grow = gcr_s[hh, pl.ds(sc, 1), :]
            qc = l2norm(conv_silu(xq_ref, cwq_ref, hh, sc)) * (HEAD_DIM ** -0.5)
            kc = l2norm(conv_silu(xk_ref, cwk_ref, hh, sc))
            s["decay"] = jnp.exp((gcol - grow) + neg_inf_mask)
            kb = kc * bcol
            egc = jnp.exp(gcol)
            s["rhs"] = jnp.concatenate([conv_silu(xv_ref, cwv_ref, hh, sc) * bcol, kb * egc], axis=1)
            s["kb"], s["kc"], s["qc"] = kb.astype(BF16), kc.astype(BF16), qc.astype(BF16)
            qd = (qc * egc).astype(BF16)
            for p in range(per):
                blk = slice(p * CHUNK, (p + 1) * CHUNK)
                wq_s[slot, hh, p, CHUNK:, :] = qd[blk, :]
            kt_s[slot, hh] = (kc * jnp.exp(glast - gcol)).astype(BF16)
        yield
        for hh in heads:
            s = st[hh]
            s["y"] = _dot_nt(s["kb"], s["kc"]) * s["decay"] * neg_strict
            s["z"] = s["y"]
            attn = (_dot_nt(s["qc"], s["kc"]) * s["decay"]).astype(BF16)
            for p in range(per):
                blk = slice(p * CHUNK, (p + 1) * CHUNK)
                at_s[slot, hh, blk, :] = attn[blk, blk]
        yield
        for m in range(n_sq):
            for hh in heads:
                s = st[hh]
                yb = s["y"].astype(BF16)
                if m > 0:
                    s["z"] = s["z"] + s["y"] + _dot(s["z"].astype(BF16), yb)
                s["y"] = _dot(yb, yb)
            yield
        for hh in heads:
            s = st[hh]
            s["z"] = s["z"] + s["y"] + _dot(s["z"].astype(BF16), s["y"].astype(BF16))
        yield
        for hh in heads:
            s = st[hh]
            uw = s["rhs"] + _dot(s["z"].astype(BF16), s["rhs"].astype(BF16))
            u_s[slot, hh] = uw[:, :HEAD_DIM]
            w = uw[:, HEAD_DIM:].astype(BF16)
            for p in range(per):
                blk = slice(p * CHUNK, (p + 1) * CHUNK)
                wq_s[slot, hh, p, :CHUNK, :] = w[blk, :]
        yield

    ng = ng_ref[...]

    def scan_rounds(sc, slot, states):
        chunk_decay = [jnp.exp(gcr_s[hh, pl.ds(sc, 1), :]) for hh in heads]
        for p in range(per):
            blk = slice(p * CHUNK, (p + 1) * CHUNK)
            rows = pl.ds(_aligned(sc * sup + p * CHUNK, CHUNK), CHUNK)
            vnb, qs = [None] * GDN_GROUP, [None] * GDN_GROUP
            for hh in heads:
                both = _dot(wq_s[slot, hh, p], states[hh].astype(BF16))
                vnb[hh] = (u_s[slot, hh, blk, :] - both[:CHUNK, :]).astype(BF16)
                qs[hh] = both[CHUNK:, :]
            yield
            for hh in heads:
                o = qs[hh] + _dot(at_s[slot, hh, blk, :], vnb[hh])
                last = p * CHUNK + CHUNK - 1
                cd = chunk_decay[hh][:, last:last + 1]
                states[hh] = states[hh] * cd + _dot_tn(kt_s[slot, hh, blk, :], vnb[hh])
                o_ref[rows, lanes(hh)] = (_rmsnorm(o, ng) * _silu(z_ref[rows, lanes(hh)])).astype(o_ref.dtype)
            yield

    def step(i, states):
        states = list(states)
        slot = lax.rem(i, 2)
        _interleave(prep_rounds(i, slot), scan_rounds(i - 1, 1 - slot, states))
        return tuple(states)

    _interleave(prep_rounds(0, 0))
    states = tuple(jnp.zeros((HEAD_DIM, HEAD_DIM), F32) for _ in heads)
    states = list(lax.fori_loop(1, n_sup, step, states))
    _interleave(scan_rounds(n_sup - 1, (n_sup - 1) % 2, states))


def gated_deltanet(qkvz, ba_rows, ba_cols, conv_w, a_log, dt_bias, norm_g, layer, batch, seq, *, name):
    sup = min(GDN_SUPER, seq)
    n_sup = seq // sup
    n_groups = GDN_HEADS // GDN_GROUP
    width = GDN_GROUP * HEAD_DIM
    col = lambda part: pl.BlockSpec((seq, width), lambda b, g: (b, part * n_groups + g))
    cw = lambda part: pl.BlockSpec((None, SHORT_CONV, width), lambda b, g: (layer, 0, part * n_groups + g))
    rows = lambda part: pl.BlockSpec((None, GDN_GROUP, n_sup, sup), lambda b, g: (b, part * n_groups + g, 0, 0))
    cols = lambda part: pl.BlockSpec((None, GDN_GROUP, sup, n_sup), lambda b, g: (b, part * n_groups + g, 0, 0))
    smem = pl.BlockSpec(memory_space=pltpu.SMEM)
    per_head = lambda shape, dt: pltpu.VMEM((GDN_GROUP,) + shape, dt)
    slots = lambda shape, dt: pltpu.VMEM((2, GDN_GROUP) + shape, dt)
    return pl.pallas_call(
        functools.partial(_gdn_kernel, layer=layer, seq=seq),
        grid=(batch, n_groups),
        in_specs=[smem, smem, col(0), col(1), col(2), col(3), cw(0), cw(1), cw(2),
                  rows(0), rows(1), cols(0), cols(1),
                  pl.BlockSpec((None, 1, HEAD_DIM), lambda b, g: (layer, 0, 0))],
        out_specs=pl.BlockSpec((seq, width), lambda b, g: (b, g)),
        out_shape=jax.ShapeDtypeStruct((batch * seq, GDN_WIDTH), BF16),
        scratch_shapes=[
            slots((sup, HEAD_DIM), F32),
            slots((sup // CHUNK, 2 * CHUNK, HEAD_DIM), BF16),
            slots((sup, HEAD_DIM), BF16),
            slots((sup, CHUNK), BF16),
            per_head((n_sup, sup), F32),
            per_head((3, sup, n_sup), F32),
        ],
        compiler_params=_params(2), name=name,
    )(a_log, dt_bias, qkvz, qkvz, qkvz, qkvz, conv_w, conv_w, conv_w,
      ba_rows, ba_rows, ba_cols, ba_cols, norm_g)


def _proj2_residual_kernel(x_ref, a1_ref, a2_ref, w1_ref, w2_ref, o_ref):
    o_ref[...] = x_ref[...] + _dot(a1_ref[...], w1_ref[...]) + _dot(a2_ref[...], w2_ref[...])


def proj2_residual(x, a1, a2, w, layer, *, tm, tn, name):
    m, n = x.shape
    k1, k2 = a1.shape[1], a2.shape[1]
    assert k1 == k2
    tm = min(tm, m)
    return pl.pallas_call(
        _proj2_residual_kernel,
        grid=(m // tm, n // tn),
        in_specs=[
            pl.BlockSpec((tm, tn), lambda i, j: (i, j)),
            pl.BlockSpec((tm, k1), lambda i, j: (i, 0)),
            pl.BlockSpec((tm, k2), lambda i, j: (i, 0)),
            pl.BlockSpec((None, k1, tn), lambda i, j: (layer, 0, j)),
            pl.BlockSpec((None, k2, tn), lambda i, j: (layer, 1, j)),
        ],
        out_specs=pl.BlockSpec((tm, tn), lambda i, j: (i, j)),
        out_shape=jax.ShapeDtypeStruct((m, n), F32),
        compiler_params=_params(2), name=name,
    )(x, a1, a2, w, w)


def _xattn_kernel(x_ref, g_ref, wq_ref, k_ref, v_ref, wo_ref, gn_ref, o_ref, hn_ref, h_s, a_s, *, tm):
    _rmsnorm_into(h_s, 0, x_ref, g_ref[...], tm)
    q = _dot(h_s[...], wq_ref[...]).astype(BF16)
    scale = HEAD_DIM ** -0.5
    cols = [slice(hd * HEAD_DIM, (hd + 1) * HEAD_DIM) for hd in range(X_HEADS)]
    logits = [_dot_nt(q[:, c], k_ref[:, c]) * scale for c in cols]
    probs = [jnp.exp(s - jnp.max(s, axis=-1, keepdims=True)) for s in logits]
    outs = [_dot(p.astype(BF16), v_ref[:, c]) for p, c in zip(probs, cols)]
    for p, o, c in zip(probs, outs, cols):
        a_s[:, c] = (o / jnp.sum(p, axis=-1, keepdims=True)).astype(BF16)
    o_ref[...] = x_ref[...] + _dot(a_s[...], wo_ref[...])
    _rmsnorm_into(hn_ref, 0, o_ref, gn_ref[...], tm)


def cross_attention_block(x, g, wq, kv, wo, g_next, layer, seq, n_mem, *, tm, name):
    m, d = x.shape
    tm = min(tm, seq)
    per_seq = seq // tm
    gain = pl.BlockSpec((None, 1, d), lambda i: (layer, 0, 0))
    rows = pl.BlockSpec((tm, d), lambda i: (i, 0))
    return pl.pallas_call(
        functools.partial(_xattn_kernel, tm=tm),
        grid=(m // tm,),
        in_specs=[
            rows, gain,
            pl.BlockSpec((None, d, X_WIDTH), lambda i: (layer, 0, 0)),
            pl.BlockSpec((n_mem, X_WIDTH), lambda i: (i // per_seq, 0)),
            pl.BlockSpec((n_mem, X_WIDTH), lambda i: (i // per_seq, 1)),
            pl.BlockSpec((None, X_WIDTH, d), lambda i: (layer, 0, 0)),
            gain,
        ],
        out_specs=[rows, rows],
        out_shape=[jax.ShapeDtypeStruct((m, d), F32), jax.ShapeDtypeStruct((m, d), BF16)],
        scratch_shapes=[pltpu.VMEM((tm, d), BF16), pltpu.VMEM((tm, X_WIDTH), BF16)],
        compiler_params=_params(1), name=name,
    )(x, g, wq, kv, kv, wo, g_next)


def _ffn_up_kernel(h_ref, halo_ref, wg_ref, wu_ref, cg_ref, cu_ref, bg_ref, bu_ref,
                   o_ref, h_s, *, tm, tiles_per_seq):
    i = pl.program_id(0)

    @pl.when(pl.program_id(1) == 0)
    def _():
        keep = jnp.where(i % tiles_per_seq == 0, 0.0, 1.0).astype(h_s.dtype)
        h_s[:HALO_ROWS, :] = halo_ref[...] * keep
        step = min(COPY_ROWS, tm)

        def body(r, carry):
            src = pl.ds(pl.multiple_of(r * step, step), step)
            dst = pl.ds(pl.multiple_of(HALO_ROWS + r * step, HALO_ROWS), step)
            h_s[dst, :] = h_ref[src, :]
            return carry

        lax.fori_loop(0, tm // step, body, 0)

    def conv(y, cw_ref, b_ref):
        cw = cw_ref[...]
        out = cw[FFN_CONV - 1:FFN_CONV, :] * y[HALO_ROWS:, :]
        for tap in range(FFN_CONV - 1):
            shift = FFN_CONV - 1 - tap
            out = out + cw[tap:tap + 1, :] * y[HALO_ROWS - shift:HALO_ROWS - shift + tm, :]
        return out + b_ref[...]

    h = h_s[...]
    gate = conv(_dot(h, wg_ref[...]), cg_ref, bg_ref)
    up = conv(_dot(h, wu_ref[...]), cu_ref, bu_ref)
    o_ref[...] = (_silu(gate) * up).astype(o_ref.dtype)


def ffn_up(h, w_up, conv_w, conv_b, layer, seq, *, tm, tn, name):
    m, d = h.shape
    d_ff = w_up.shape[-1] // 2
    tm = min(tm, seq)
    assert d_ff % tn == 0 and seq % tm == 0 and tm % HALO_ROWS == 0
    nj = d_ff // tn
    halo_blocks = tm // HALO_ROWS
    return pl.pallas_call(
        functools.partial(_ffn_up_kernel, tm=tm, tiles_per_seq=seq // tm),
        grid=(m // tm, nj),
        in_specs=[
            pl.BlockSpec((tm, d), lambda i, j: (i, 0)),
            pl.BlockSpec((HALO_ROWS, d), lambda i, j: (jnp.maximum(i * halo_blocks - 1, 0), 0)),
            pl.BlockSpec((None, d, tn), lambda i, j: (layer, 0, j)),
            pl.BlockSpec((None, d, tn), lambda i, j: (layer, 0, nj + j)),
            pl.BlockSpec((None, FFN_CONV, tn), lambda i, j: (layer, 0, j)),
            pl.BlockSpec((None, FFN_CONV, tn), lambda i, j: (layer, 0, nj + j)),
            pl.BlockSpec((None, 1, tn), lambda i, j: (layer, 0, j)),
            pl.BlockSpec((None, 1, tn), lambda i, j: (layer, 0, nj + j)),
        ],
        out_specs=pl.BlockSpec((tm, tn), lambda i, j: (i, j)),
        out_shape=jax.ShapeDtypeStruct((m, d_ff), BF16),
        scratch_shapes=[pltpu.VMEM((tm + HALO_ROWS, d), BF16)],
        compiler_params=_params(2), name=name,
    )(h, h, w_up, w_up, conv_w, conv_w, conv_b, conv_b)


def _matmul_residual_kernel(x_ref, a_ref, w_ref, o_ref):
    o_ref[...] = x_ref[...] + _dot(a_ref[...], w_ref[...])


def matmul_residual(x, a, w, layer, *, tm, tn, name):
    m, n = x.shape
    k = a.shape[1]
    tm = min(tm, m)
    return pl.pallas_call(
        _matmul_residual_kernel,
        grid=(m // tm, n // tn),
        in_specs=[
            pl.BlockSpec((tm, tn), lambda i, j: (i, j)),
            pl.BlockSpec((tm, k), lambda i, j: (i, 0)),
            pl.BlockSpec((None, k, tn), lambda i, j: (layer, 0, j)),
        ],
        out_specs=pl.BlockSpec((tm, tn), lambda i, j: (i, j)),
        out_shape=jax.ShapeDtypeStruct((m, n), F32),
        compiler_params=_params(2), name=name,
    )(x, a, w)


def _final_norm_kernel(x_ref, g_ref, o_ref, *, tm):
    _rmsnorm_into(o_ref, 0, x_ref, g_ref[...], tm)


def final_norm(x, g, *, tm, name):
    m, d = x.shape
    tm = min(tm, m)
    return pl.pallas_call(
        functools.partial(_final_norm_kernel, tm=tm),
        grid=(m // tm,),
        in_specs=[pl.BlockSpec((tm, d), lambda i: (i, 0)), pl.BlockSpec((1, d), lambda i: (0, 0))],
        out_specs=pl.BlockSpec((tm, d), lambda i: (i, 0)),
        out_shape=jax.ShapeDtypeStruct((m, d), F32),
        compiler_params=_params(1), name=name,
    )(x, g)


def kernel(x, mem, mix_norm, w_in, gdn_conv, gdn_a_log, gdn_dt_bias, gdn_norm, w_out, xattn_norm,
           mem_norm, w_xq, w_xkv, w_xo, ffn_norm, w_up, ffn_conv, ffn_conv_bias, w_down, final_norm_g):
    batch, seq, d_model = x.shape
    n_mem = mem.shape[1]
    depth = w_in.shape[0]
    sup = min(GDN_SUPER, seq)
    n_gate = 2 * GDN_HEADS

    w_out_b = w_out.astype(BF16)
    w_down_b = w_down.astype(BF16)
    w_in_t = jnp.swapaxes(w_in, 1, 2)
    row3 = lambda a: a.reshape(depth, 1, a.shape[-1])
    mix_g, xat_g, mem_g, ffn_g = row3(mix_norm), row3(xattn_norm), row3(mem_norm), row3(ffn_norm)
    gdn_g = row3(gdn_norm)
    conv_b = row3(ffn_conv_bias)

    xf = x.reshape(batch * seq, d_model)
    memf = mem.reshape(batch * n_mem, d_model)

    t = TILES
    for l in range(depth):
        sb_qkv, gdn_qkvz, gates = in_projection(xf, mix_g, w_in_t, l, 3 * SB_WIDTH, 4 * GDN_WIDTH,
                                                **t["in_proj"], name=f"inproj_{l}")
        sb_out = sb_attention(sb_qkv, batch, seq, name=f"sb_attn_{l}")
        ba = gates.reshape(batch, seq, n_gate).transpose(0, 2, 1)
        ba_rows = ba.reshape(batch, n_gate, seq // sup, sup)
        ba_cols = ba_rows.transpose(0, 1, 3, 2)
        gdn_out = gated_deltanet(gdn_qkvz, ba_rows, ba_cols, gdn_conv, gdn_a_log, gdn_dt_bias, gdn_g,
                                 l, batch, seq, name=f"gdn_{l}")
        xf = proj2_residual(xf, sb_out, gdn_out, w_out_b, l, **t["out_proj"], name=f"outproj_{l}")

        kv = norm_matmul(memf, mem_g, w_xkv, l, BF16, **t["mem_kv"], name=f"mem_kv_{l}")
        xf, h_ffn = cross_attention_block(xf, xat_g, w_xq, kv, w_xo, ffn_g, l, seq, n_mem,
                                          **t["xattn"], name=f"xattn_{l}")

        act = ffn_up(h_ffn, w_up, ffn_conv, conv_b, l, seq, **t["ffn_up"], name=f"ffn_up_{l}")
        xf = matmul_residual(xf, act, w_down_b, l, **t["ffn_down"], name=f"ffn_down_{l}")

    out = final_norm(xf, final_norm_g.reshape(1, d_model), **t["final_norm"], name="final_norm")
    return out.reshape(batch, seq, d_model)
```
